```python
import math
import jax, jax.numpy as jnp
from jax import lax
import numpy as np

D_MODEL = 4096
BATCH = 4
SEQ = 2048
DEPTH = 1
DEC_BATCH = 8
DEC_SEQ = 32
PAST_LEN = 4096

CHUNK = 64
Q_BLOCK = 128
RWKV_HEAD = 64
RWKV_WIDTH = D_MODEL // 2
RWKV_HEADS = RWKV_WIDTH // RWKV_HEAD
DECAY_LORA = max(32, int(round(math.sqrt(D_MODEL) * 1.8 / 32)) * 32)
AAA_LORA = max(32, int(round(math.sqrt(D_MODEL) * 1.8 / 32)) * 32)
GATE_LORA = max(32, int(round(D_MODEL ** 0.6 * 0.6 / 32)) * 32)
RWKV_SHIFT_WIDTH = 3 * RWKV_WIDTH + DECAY_LORA + AAA_LORA + GATE_LORA
GN_EPS = RWKV_HEAD * 1e-5
DIFF_DH = 64
DIFF_HEADS = D_MODEL // (4 * DIFF_DH)
DIFF_WIDTH = DIFF_HEADS * 2 * DIFF_DH
ROT_DIM = DIFF_DH // 4
ROPE_THETA = 500000.0
GATE_WIDTH = 2 * D_MODEL
N_IN = RWKV_SHIFT_WIDTH + 3 * DIFF_WIDTH + GATE_WIDTH
D_FF = 4 * D_MODEL
PLE_DIM = 256
RMS_EPS = 1e-6
NEG_INF = -1e30

kernel_name = "rwkv7_diffattn_gated_streaming_encoder"


def _rmsnorm(x, g):
    xf = x.astype(jnp.float32)
    y = xf * lax.rsqrt(jnp.mean(xf * xf, axis=-1, keepdims=True) + RMS_EPS)
    return (y * g.astype(jnp.float32)).astype(x.dtype)


def _partial_rope(x, pos):
    half = ROT_DIM // 2
    inv_freq = jnp.power(jnp.float32(ROPE_THETA), -jnp.arange(0, ROT_DIM, 2, dtype=jnp.float32) / ROT_DIM)
    ang = pos.astype(jnp.float32)[:, None] * inv_freq[None, :]
    cos = jnp.cos(ang)[:, None, None, :]
    sin = jnp.sin(ang)[:, None, None, :]
    xf = x[..., :ROT_DIM].astype(jnp.float32)
    x1, x2 = xf[..., :half], xf[..., half:]
    rot = jnp.concatenate([x1 * cos - x2 * sin, x1 * sin + x2 * cos], axis=-1).astype(x.dtype)
    return jnp.concatenate([rot, x[..., ROT_DIM:]], axis=-1)


def _token_shift(P, prev, mu):
    shifted = jnp.concatenate([prev.astype(P.dtype), P[:, :-1]], axis=1)
    return P + (shifted - P) * mu


def _rwkv7(Ps, s0, w_decay_up, w0, a_up, a0, g_up, k_k, k_a, r_k, lnx_g, lnx_b):
    B, T, _ = Ps.shape
    f32 = lambda a: a.astype(jnp.float32)
    Pf = f32(Ps)
    W = RWKV_WIDTH
    o1 = 3 * W
    o2 = o1 + DECAY_LORA
    o3 = o2 + AAA_LORA
    r, k, v = Pf[..., :W], Pf[..., W:2 * W], Pf[..., 2 * W:3 * W]
    pw, pa, pg = Pf[..., o1:o2], Pf[..., o2:o3], Pf[..., o3:]
    z = f32(w0) + jnp.tanh(pw) @ f32(w_decay_up)
    decay = jnp.exp(-jnp.exp(-jax.nn.softplus(-z) - 0.5))
    a = jax.nn.sigmoid(f32(a0) + pa @ f32(a_up))
    g = jax.nn.sigmoid(pg) @ f32(g_up)
    kk = k * f32(k_k)
    k = k * (1.0 + (a - 1.0) * f32(k_a))
    heads = lambda t: t.reshape(B, T, RWKV_HEADS, RWKV_HEAD)
    r, k, v, kk, a, decay = heads(r), heads(k), heads(v), heads(kk), heads(a), heads(decay)
    kk = kk * lax.rsqrt(jnp.sum(kk * kk, axis=-1, keepdims=True) + 1e-12)
    xs = tuple(jnp.moveaxis(t, 1, 0) for t in (r, decay, k, v, -kk, kk * a))

    def step(S, inp):
        r_t, w_t, k_t, v_t, a_t, b_t = inp
        sa = jnp.einsum('bhvk,bhk->bhv', S, a_t)
        S = S * w_t[:, :, None, :] + sa[..., None] * b_t[:, :, None, :] + v_t[..., None] * k_t[:, :, None, :]
        return S, jnp.einsum('bhvk,bhk->bhv', S, r_t)

    s_final, o = lax.scan(step, f32(s0), xs)
    o = jnp.moveaxis(o, 0, 1)
    mu = jnp.mean(o, axis=-1, keepdims=True)
    var = jnp.mean(jnp.square(o - mu), axis=-1, keepdims=True)
    gn = (o - mu) * lax.rsqrt(var + GN_EPS) * f32(lnx_g).reshape(RWKV_HEADS, RWKV_HEAD) + f32(lnx_b).reshape(RWKV_HEADS, RWKV_HEAD)
    bonus = jnp.sum(r * k * f32(r_k), axis=-1, keepdims=True) * v
    y = (gn + bonus).reshape(B, T, W) * g
    return y, s_final


def _diff_attn_block(q, k, v, q_pos, k_pos, lam):
    s = jnp.einsum('bqhcd,bkhcd->bhcqk', q, k).astype(jnp.float32) * (DIFF_DH ** -0.5)
    visible = (q_pos // CHUNK)[:, None] >= (k_pos // CHUNK)[None, :]
    s = jnp.where(visible, s, NEG_INF)
    pr = jax.nn.softmax(s, axis=-1)
    att = pr[:, :, 0] - lam * pr[:, :, 1]
    return jnp.einsum('bhqk,bkhd->bqhd', att, v.astype(jnp.float32))


def _layer(x, pe, pos, shift_prev, s0, k_past, v_past, lam_init,
           norm1_g, w_in, tm_mu, w_decay_up, w0, a_up, a0, g_up, k_k, k_a, r_k, lnx_g, lnx_b,
           q_norm_g, k_norm_g, lam_q1, lam_k1, lam_q2, lam_k2, subln_g,
           w_branch_rwkv, w_branch_diff, w_out, norm2_g, w_up, w_down,
           ple_norm_g, w_ple_gate, w_ple_proj):
    B, T, _ = x.shape
    h = _rmsnorm(x, norm1_g)
    P = h @ w_in
    c1 = RWKV_SHIFT_WIDTH
    c2 = c1 + DIFF_WIDTH
    c3 = c2 + DIFF_WIDTH
    c4 = c3 + DIFF_WIDTH
    P_r = P[..., :c1]
    y_r, s_new = _rwkv7(_token_shift(P_r, shift_prev, tm_mu), s0, w_decay_up, w0, a_up, a0, g_up,
                        k_k, k_a, r_k, lnx_g, lnx_b)

    q = P[..., c1:c2].reshape(B, T, DIFF_HEADS, 2, DIFF_DH)
    k = P[..., c2:c3].reshape(B, T, DIFF_HEADS, 2, DIFF_DH)
    v = P[..., c3:c4].reshape(B, T, DIFF_HEADS, 2 * DIFF_DH)
    q = _partial_rope(_rmsnorm(q, q_norm_g), pos)
    k = _partial_rope(_rmsnorm(k, k_norm_g), pos)
    f32 = lambda a: a.astype(jnp.float32)
    lam = jnp.exp(jnp.sum(f32(lam_q1) * f32(lam_k1))) - jnp.exp(jnp.sum(f32(lam_q2) * f32(lam_k2))) + lam_init
    if k_past is None:
        nb = T // Q_BLOCK
        qb = jnp.moveaxis(q.reshape(B, nb, Q_BLOCK, DIFF_HEADS, 2, DIFF_DH), 1, 0)
        pb = pos.reshape(nb, Q_BLOCK)
        ob = lax.map(lambda blk: _diff_attn_block(blk[0], k, v, blk[1], pos, lam), (qb, pb))
        o = jnp.moveaxis(ob, 0, 1).reshape(B, T, DIFF_HEADS, 2 * DIFF_DH)
    else:
        past = k_past.shape[1]
        k_all = jnp.concatenate([k_past.reshape(B, past, DIFF_HEADS, 2, DIFF_DH).astype(k.dtype), k], axis=1)
        v_all = jnp.concatenate([v_past.astype(v.dtype), v], axis=1)
        k_pos = jnp.concatenate([jnp.arange(past, dtype=jnp.int32), pos])
        o = _diff_attn_block(q, k_all, v_all, pos, k_pos, lam)
    y_d = (_rmsnorm(o, subln_g) * (1.0 - lam_init)).reshape(B, T, DIFF_WIDTH).astype(x.dtype)

    gates = jax.nn.sigmoid(f32(P[..., c4:]))
    merged = gates[..., :D_MODEL] * (y_r.astype(x.dtype) @ w_branch_rwkv) + gates[..., D_MODEL:] * (y_d @ w_branch_diff)
    x1 = x + merged.astype(x.dtype) @ w_out
    u = jnp.square(jax.nn.relu(_rmsnorm(x1, norm2_g) @ w_up))
    x2 = x1 + u @ w_down
    x3 = x2 + (jax.nn.sigmoid(_rmsnorm(x2, ple_norm_g) @ w_ple_gate) * (pe.astype(x.dtype) @ w_ple_proj)).astype(x.dtype)
    return x3, k.reshape(B, T, DIFF_HEADS, 2 * DIFF_DH), v, s_new.astype(s0.dtype), P_r[:, -1:]


def setup_inputs(seed: int = 0) -> dict:
    key = jax.random.key(seed)
    ks = iter(jax.random.split(key, 64))
    nrm = lambda shape, scale=1.0: jax.random.normal(next(ks), shape, jnp.float32) * scale
    gain = lambda shape: 1.0 + nrm(shape, 0.02)
    L = DEPTH
    return {
        "x_prompt": nrm((BATCH, SEQ, D_MODEL)),
        "x_sample": nrm((DEC_BATCH, DEC_SEQ, D_MODEL)),
        "p_prompt": nrm((L, BATCH, SEQ, PLE_DIM)),
        "p_sample": nrm((L, DEC_BATCH, DEC_SEQ, PLE_DIM)),
        "cache_k": nrm((L, DEC_BATCH, PAST_LEN, DIFF_HEADS, 2 * DIFF_DH)),
        "cache_v": nrm((L, DEC_BATCH, PAST_LEN, DIFF_HEADS, 2 * DIFF_DH)),
        "state_rwkv": nrm((L, DEC_BATCH, RWKV_HEADS, RWKV_HEAD, RWKV_HEAD), 0.5),
        "state_shift": nrm((L, DEC_BATCH, 1, RWKV_SHIFT_WIDTH)),
        "norm1_g": gain((L, D_MODEL)),
        "w_in": nrm((L, D_MODEL, N_IN), D_MODEL ** -0.5),
        "tm_mu": jax.random.uniform(next(ks), (L, RWKV_SHIFT_WIDTH), jnp.float32),
        "w_decay_up": nrm((L, DECAY_LORA, RWKV_WIDTH), 0.05),
        "w0": nrm((L, RWKV_WIDTH), 0.5),
        "a_up": nrm((L, AAA_LORA, RWKV_WIDTH), 0.05),
        "a0": nrm((L, RWKV_WIDTH), 0.1),
        "g_up": nrm((L, GATE_LORA, RWKV_WIDTH), GATE_LORA ** -0.5),
        "k_k": 0.85 + nrm((L, RWKV_WIDTH), 0.02),
        "k_a": gain((L, RWKV_WIDTH)),
        "r_k": nrm((L, RWKV_HEADS, RWKV_HEAD), 0.1),
        "lnx_g": gain((L, RWKV_WIDTH)),
        "lnx_b": nrm((L, RWKV_WIDTH), 0.01),
        "q_norm_g": gain((L, DIFF_DH)),
        "k_norm_g": gain((L, DIFF_DH)),
        "lam_q1": nrm((L, DIFF_DH), 0.1),
        "lam_k1": nrm((L, DIFF_DH), 0.1),
        "lam_q2": nrm((L, DIFF_DH), 0.1),
        "lam_k2": nrm((L, DIFF_DH), 0.1),
        "subln_g": gain((L, 2 * DIFF_DH)),
        "w_branch_rwkv": nrm((L, RWKV_WIDTH, D_MODEL), RWKV_WIDTH ** -0.5),
        "w_branch_diff": nrm((L, DIFF_WIDTH, D_MODEL), DIFF_WIDTH ** -0.5),
        "w_out": nrm((L, D_MODEL, D_MODEL), D_MODEL ** -0.5),
        "norm2_g": gain((L, D_MODEL)),
        "w_up": nrm((L, D_MODEL, D_FF), D_MODEL ** -0.5),
        "w_down": nrm((L, D_FF, D_MODEL), 0.5 * D_FF ** -0.5),
        "ple_norm_g": gain((L, D_MODEL)),
        "w_ple_gate": nrm((L, D_MODEL, D_MODEL), D_MODEL ** -0.5),
        "w_ple_proj": nrm((L, PLE_DIM, D_MODEL), PLE_DIM ** -0.5),
    }


def reference(x_prompt, x_sample, p_prompt, p_sample, cache_k, cache_v, state_rwkv, state_shift,
              norm1_g, w_in, tm_mu, w_decay_up, w0, a_up, a0, g_up, k_k, k_a, r_k, lnx_g, lnx_b,
              q_norm_g, k_norm_g, lam_q1, lam_k1, lam_q2, lam_k2, subln_g,
              w_branch_rwkv, w_branch_diff, w_out, norm2_g, w_up, w_down,
              ple_norm_g, w_ple_gate, w_ple_proj):
    Bp, Tp, _ = x_prompt.shape
    Bs, Ts, _ = x_sample.shape
    past = cache_k.shape[2]
    pos_p = jnp.arange(Tp, dtype=jnp.int32)
    pos_s = past + jnp.arange(Ts, dtype=jnp.int32)
    hp, hs = x_prompt, x_sample
    kp_l, vp_l, sp_l, shp_l = [], [], [], []
    ks_l, vs_l, ss_l, shs_l = [], [], [], []
    for i in range(DEPTH):
        lw = dict(norm1_g=norm1_g[i], w_in=w_in[i], tm_mu=tm_mu[i], w_decay_up=w_decay_up[i], w0=w0[i],
                  a_up=a_up[i], a0=a0[i], g_up=g_up[i], k_k=k_k[i], k_a=k_a[i], r_k=r_k[i],
                  lnx_g=lnx_g[i], lnx_b=lnx_b[i], q_norm_g=q_norm_g[i], k_norm_g=k_norm_g[i],
                  lam_q1=lam_q1[i], lam_k1=lam_k1[i], lam_q2=lam_q2[i], lam_k2=lam_k2[i], subln_g=subln_g[i],
                  w_branch_rwkv=w_branch_rwkv[i], w_branch_diff=w_branch_diff[i], w_out=w_out[i],
                  norm2_g=norm2_g[i], w_up=w_up[i], w_down=w_down[i], ple_norm_g=ple_norm_g[i],
                  w_ple_gate=w_ple_gate[i], w_ple_proj=w_ple_proj[i])
        lam_init = 0.8 - 0.6 * math.exp(-0.3 * i)
        shift0 = jnp.zeros((Bp, 1, RWKV_SHIFT_WIDTH), hp.dtype)
        s_zero = jnp.zeros((Bp, RWKV_HEADS, RWKV_HEAD, RWKV_HEAD), state_rwkv.dtype)
        hp, kp, vp, sp, shp = _layer(hp, p_prompt[i], pos_p, shift0, s_zero, None, None, lam_init, **lw)
        hs, kn, vn, sn, shn = _layer(hs, p_sample[i], pos_s, state_shift[i], state_rwkv[i],
                                     cache_k[i], cache_v[i], lam_init, **lw)
        kp_l.append(kp); vp_l.append(vp); sp_l.append(sp); shp_l.append(shp)
        ks_l.append(kn); vs_l.append(vn); ss_l.append(sn); shs_l.append(shn)
    return (hp, hs,
            jnp.stack(kp_l, 0), jnp.stack(vp_l, 0), jnp.stack(sp_l, 0), jnp.stack(shp_l, 0),
            jnp.stack(ks_l, 0), jnp.stack(vs_l, 0), jnp.stack(ss_l, 0), jnp.stack(shs_l, 0))
```

```python
import functools
import math

import jax
import jax.numpy as jnp
from jax import lax
from jax.experimental import pallas as pl
from jax.experimental.pallas import tpu as pltpu

F32 = jnp.float32
BF16 = jnp.bfloat16
HI = lax.Precision.HIGHEST

LANES = 128
HEAD = 64
RMS_EPS = 1e-6
GN_EPS = HEAD * 1e-5
NEG_INF = -1e30
CHUNK = 64
ROT_DIM = HEAD // 4
ROPE_THETA = 500000.0
SCAN_L = 128
SEG_ALIGN = 512
VMEM_LIMIT = 56 * 1024 * 1024
NT = (((1,), (1,)), ((), ()))
TN = (((0,), (0,)), ((), ()))


def _round_up(n, m):
    return (n + m - 1) // m * m


def _pick(n, target, mult):
    best = None
    for d in range(mult, min(n, target) + 1, mult):
        if n % d == 0:
            best = d
    assert best is not None, (n, target, mult)
    return best


def _cp(sem):
    return pltpu.CompilerParams(dimension_semantics=sem, vmem_limit_bytes=VMEM_LIMIT)


def _bdot(a, b):
    return jnp.dot(a.astype(BF16), b.astype(BF16), preferred_element_type=F32)


def _seg_allsum(x, seg):
    lane = lax.broadcasted_iota(jnp.int32, x.shape, 1)
    s = 1
    while s < seg:
        up = pltpu.roll(x, LANES - s, axis=1)
        dn = pltpu.roll(x, s, axis=1)
        x = x + jnp.where((lane & s) == 0, up, dn)
        s *= 2
    return x


def _rmsnorm_kernel(x_ref, g_ref, o_ref):
    x = x_ref[...]
    ms = jnp.mean(x * x, axis=-1, keepdims=True)
    o_ref[...] = (x * lax.rsqrt(ms + RMS_EPS) * g_ref[...]).astype(o_ref.dtype)


def _rmsnorm(x, g):
    M, D = x.shape
    tr = _pick(M, 256, 16)
    return pl.pallas_call(
        _rmsnorm_kernel, grid=(M // tr,),
        in_specs=[pl.BlockSpec((tr, D), lambda i: (i, 0)), pl.BlockSpec((1, D), lambda i: (0, 0))],
        out_specs=pl.BlockSpec((tr, D), lambda i: (i, 0)),
        out_shape=jax.ShapeDtypeStruct((M, D), BF16),
        compiler_params=_cp(("parallel",)), name="rmsnorm")(x, g.reshape(1, D))


def _mm_kernel(a_ref, b_ref, o_ref):
    o_ref[...] = _bdot(a_ref[...], b_ref[...]).astype(o_ref.dtype)


def _mm_relu2_kernel(a_ref, b_ref, o_ref):
    p = jnp.maximum(_bdot(a_ref[...], b_ref[...]), 0.0)
    o_ref[...] = (p * p).astype(o_ref.dtype)


def _mm_res_kernel(a_ref, b_ref, r_ref, o_ref):
    o_ref[...] = r_ref[...] + _bdot(a_ref[...], b_ref[...])


def _mm_fullk(kernel, a, b, out_dtype, res=None, tn_target=256, name="mm"):
    M, K = a.shape
    N = b.shape[1]
    tm = _pick(M, 1408, 16)
    tn = _pick(N, tn_target, LANES)
    in_specs = [pl.BlockSpec((tm, K), lambda i, j: (i, 0)), pl.BlockSpec((K, tn), lambda i, j: (0, j))]
    args = [a, b]
    if res is not None:
        in_specs.append(pl.BlockSpec((tm, tn), lambda i, j: (i, j)))
        args.append(res)
    return pl.pallas_call(
        kernel, grid=(M // tm, N // tn), in_specs=in_specs,
        out_specs=pl.BlockSpec((tm, tn), lambda i, j: (i, j)),
        out_shape=jax.ShapeDtypeStruct((M, N), out_dtype),
        compiler_params=_cp(("parallel", "parallel")), name=name)(*args)


def _mm_res_kloop_kernel(a_ref, b_ref, r_ref, o_ref):
    p = _bdot(a_ref[...], b_ref[...])

    @pl.when(pl.program_id(2) == 0)
    def _():
        o_ref[...] = r_ref[...] + p

    @pl.when(pl.program_id(2) > 0)
    def _():
        o_ref[...] += p


def _mm_res_kloop(a, b, res, name):
    M, K = a.shape
    N = b.shape[1]
    tm = _pick(M, 1408, 16)
    tn = _pick(N, 512, LANES)
    tk = _pick(K, 2048, LANES)
    return pl.pallas_call(
        _mm_res_kloop_kernel, grid=(M // tm, N // tn, K // tk),
        in_specs=[pl.BlockSpec((tm, tk), lambda i, j, k: (i, k)),
                  pl.BlockSpec((tk, tn), lambda i, j, k: (k, j)),
                  pl.BlockSpec((tm, tn), lambda i, j, k: (i, j))],
        out_specs=pl.BlockSpec((tm, tn), lambda i, j, k: (i, j)),
        out_shape=jax.ShapeDtypeStruct((M, N), F32),
        compiler_params=_cp(("parallel", "parallel", "arbitrary")), name=name)(a, b, res)


def _merge_kernel(yr_ref, yd_ref, wr_ref, wd_ref, gr_ref, gd_ref, o_ref):
    pr = _bdot(yr_ref[...], wr_ref[...])
    pd = _bdot(yd_ref[...], wd_ref[...])
    o_ref[...] = (jax.nn.sigmoid(gr_ref[...]) * pr + jax.nn.sigmoid(gd_ref[...]) * pd).astype(o_ref.dtype)


def _merge(y_r, y_d, w_r, w_d, proj, gate_off):
    M, Kr = y_r.shape
    Kd = y_d.shape[1]
    D = w_r.shape[1]
    tm = _pick(M, 1408, 16)
    tn = _pick(D, 256, LANES)
    g0 = gate_off // tn
    g1 = (gate_off + D) // tn
    return pl.pallas_call(
        _merge_kernel, grid=(M // tm, D // tn),
        in_specs=[pl.BlockSpec((tm, Kr), lambda i, j: (i, 0)),
                  pl.BlockSpec((tm, Kd), lambda i, j: (i, 0)),
                  pl.BlockSpec((Kr, tn), lambda i, j: (0, j)),
                  pl.BlockSpec((Kd, tn), lambda i, j: (0, j)),
                  pl.BlockSpec((tm, tn), lambda i, j: (i, g0 + j)),
                  pl.BlockSpec((tm, tn), lambda i, j: (i, g1 + j))],
        out_specs=pl.BlockSpec((tm, tn), lambda i, j: (i, j)),
        out_shape=jax.ShapeDtypeStruct((M, D), BF16),
        compiler_params=_cp(("parallel", "parallel")), name="branch_merge")(y_r, y_d, w_r, w_d, proj, proj)


def _ple_kernel(h_ref, wg_ref, pe_ref, wp_ref, x_ref, o_ref):
    gate = jax.nn.sigmoid(_bdot(h_ref[...], wg_ref[...]))
    o_ref[...] = x_ref[...] + gate * _bdot(pe_ref[...], wp_ref[...])


def _ple(h, x, pe, w_gate, w_proj, row_off, rows):
    D = x.shape[1]
    P = pe.shape[1]
    tm = _pick(rows, 1024, 16)
    tn = _pick(D, 256, LANES)
    assert row_off % tm == 0
    r0 = row_off // tm
    return pl.pallas_call(
        _ple_kernel, grid=(rows // tm, D // tn),
        in_specs=[pl.BlockSpec((tm, D), lambda i, j: (r0 + i, 0)),
                  pl.BlockSpec((D, tn), lambda i, j: (0, j)),
                  pl.BlockSpec((tm, P), lambda i, j: (i, 0)),
                  pl.BlockSpec((P, tn), lambda i, j: (0, j)),
                  pl.BlockSpec((tm, tn), lambda i, j: (r0 + i, j))],
        out_specs=pl.BlockSpec((tm, tn), lambda i, j: (i, j)),
        out_shape=jax.ShapeDtypeStruct((rows, D), F32),
        compiler_params=_cp(("parallel", "parallel")), name="ple")(h, w_gate, pe, w_proj, x)


def _prep_kernel(p_ref, pprev_ref, sp_ref, mu_ref, w0_ref, a0_ref, kk_ref, ka_ref, rk_ref,
                 wd_ref, wa_ref, wg_ref,
                 r_o, k_o, v_o, lw_o, kkn_o, a_o, g_o, bg_o, *, RW, win_w, win_a, win_g):
    tb = p_ref.shape[0]
    first_row = lax.broadcasted_iota(jnp.int32, (tb, 1), 0) == 0
    prev_row = jnp.where(pl.program_id(1) == 0, sp_ref[0], pprev_ref[7:8, :])

    def shifted(c0, c1):
        pc = p_ref[:, c0:c1]
        sh = jnp.where(first_row, prev_row[:, c0:c1], pltpu.roll(pc, 1, axis=0))
        return pc + (sh - pc) * mu_ref[:, c0:c1]

    tpw = jnp.tanh(shifted(*win_w)).astype(BF16)
    pa = shifted(*win_a).astype(BF16)
    spg = jax.nn.sigmoid(shifted(*win_g)).astype(BF16)
    for c in range(0, RW, LANES):
        cs = slice(c, c + LANES)
        r = shifted(c, c + LANES)
        k = shifted(RW + c, RW + c + LANES)
        v = shifted(2 * RW + c, 2 * RW + c + LANES)
        z = w0_ref[:, cs] + _bdot(tpw, wd_ref[:, cs])
        lw = -math.exp(-0.5) * jax.nn.sigmoid(z)
        a = jax.nn.sigmoid(a0_ref[:, cs] + _bdot(pa, wa_ref[:, cs]))
        g = _bdot(spg, wg_ref[:, cs])
        kk = k * kk_ref[:, cs]
        k2 = k * (1.0 + (a - 1.0) * ka_ref[:, cs])
        kkn = kk * lax.rsqrt(_seg_allsum(kk * kk, HEAD) + 1e-12)
        bonus = _seg_allsum(r * k2 * rk_ref[:, cs], HEAD) * v
        r_o[:, cs] = r
        k_o[:, cs] = k2
        v_o[:, cs] = v
        lw_o[:, cs] = lw
        kkn_o[:, cs] = kkn
        a_o[:, cs] = a
        g_o[:, cs] = g
        bg_o[:, cs] = bonus * g


def _lora_window(start, width, w):
    s0 = start // LANES * LANES
    s1 = _round_up(start + width, LANES)
    wp = jnp.zeros((s1 - s0, w.shape[1]), w.dtype).at[start - s0:start - s0 + width].set(w)
    return (s0, s1), wp


def _rwkv_prep(proj, row_off, B, T, SWp, shift_prev, mu, w0, a0, k_k, k_a, r_k, w_decay_up, a_up, g_up):
    RW = w0.shape[-1]
    DL, AL, GL = w_decay_up.shape[0], a_up.shape[0], g_up.shape[0]
    tb = _pick(T, 256, 8)
    nb = T // tb
    assert row_off % tb == 0
    b0 = row_off // tb
    win_w, wd = _lora_window(3 * RW, DL, w_decay_up)
    win_a, wa = _lora_window(3 * RW + DL, AL, a_up)
    win_g, wg = _lora_window(3 * RW + DL + AL, GL, g_up)
    row = lambda x: x.reshape(1, -1)
    vec = lambda: pl.BlockSpec((1, RW), lambda b, t: (0, 0))
    full = lambda x: pl.BlockSpec(x.shape, lambda b, t: (0, 0))
    blk = pl.BlockSpec((tb, RW), lambda b, t: (b * nb + t, 0))
    outs = pl.pallas_call(
        functools.partial(_prep_kernel, RW=RW, win_w=win_w, win_a=win_a, win_g=win_g),
        grid=(B, nb),
        in_specs=[pl.BlockSpec((tb, SWp), lambda b, t: (b0 + b * nb + t, 0)),
                  pl.BlockSpec((8, SWp), lambda b, t: (jnp.maximum((b0 + b * nb + t) * (tb // 8) - 1, 0), 0)),
                  pl.BlockSpec((1, 1, SWp), lambda b, t: (b, 0, 0)),
                  pl.BlockSpec((1, SWp), lambda b, t: (0, 0)),
                  vec(), vec(), vec(), vec(), vec(), full(wd), full(wa), full(wg)],
        out_specs=[blk] * 8,
        out_shape=[jax.ShapeDtypeStruct((B * T, RW), F32)] * 8,
        compiler_params=_cp(("parallel", "arbitrary")), name="rwkv_prep",
    )(proj, proj, shift_prev, row(mu), row(w0), row(a0), row(k_k), row(k_a), row(r_k), wd, wa, wg)
    return outs


def _scan_kernel(r_ref, k_ref, v_ref, lw_ref, kkn_ref, a_ref, g_ref, bg_ref, lng_ref, lnb_ref, s0_ref,
                 y_ref, so_ref, sbd_ref, *, PB):
    c = pl.program_id(2)
    L = r_ref.shape[0]
    lane = lax.broadcasted_iota(jnp.int32, (L, LANES), 1)
    lane2 = lax.broadcasted_iota(jnp.int32, (2 * L, LANES), 1)
    row = lax.broadcasted_iota(jnp.int32, (L, L), 0)
    col = lax.broadcasted_iota(jnp.int32, (L, L), 1)
    incl = row >= col
    strict = row > col
    tri = incl.astype(F32)
    eye = (row == col).astype(F32)
    srow = lax.broadcasted_iota(jnp.int32, (LANES, LANES), 0)
    scol = lax.broadcasted_iota(jnp.int32, (LANES, LANES), 1)
    same_head = (srow >= HEAD) == (scol >= HEAD)
    dot = functools.partial(jnp.dot, precision=HI, preferred_element_type=F32)
    dg = functools.partial(lax.dot_general, precision=HI, preferred_element_type=F32)

    @pl.when(c == 0)
    def _():
        sbd_ref[...] = s0_ref[0]

    for p in range(PB):
        sl = slice(p * LANES, (p + 1) * LANES)
        r, k2, v, lw = r_ref[:, sl], k_ref[:, sl], v_ref[:, sl], lw_ref[:, sl]
        kkn, alr = kkn_ref[:, sl], a_ref[:, sl]
        lc = dot(tri, lw)
        lc_last = lc[L - 1:L, :]
        g_in, g_prev, g_inv = jnp.exp(lc), jnp.exp(lc - lw), jnp.exp(-lc)
        ratio = jnp.exp(lc_last - lc)
        a = -kkn
        b = kkn * alr
        At, Rt, Bt, Kt = a * g_prev, r * g_in, b * g_inv, k2 * g_inv
        X = jnp.concatenate([At, Rt], axis=0)
        Y = jnp.concatenate([Bt, Kt], axis=0)
        W1 = jnp.zeros((L, LANES), F32)
        W2 = jnp.zeros((L, LANES), F32)
        Op = jnp.zeros((L, LANES), F32)
        arbs = []
        for e in range(2):
            he = (lane >= HEAD) if e else (lane < HEAD)
            he2 = (lane2 >= HEAD) if e else (lane2 < HEAD)
            G = dg(jnp.where(he2, X, 0.0), Y, NT)
            Nab = jnp.where(strict, G[:L, :L], 0.0)
            Aak = jnp.where(strict, G[:L, L:], 0.0)
            arbs.append(jnp.where(incl, G[L:, :L], 0.0))
            Ark = jnp.where(incl, G[L:, L:], 0.0)
            Tm = eye + Nab
            Np = Nab
            lev = 1
            while lev * 2 < L:
                Np = dot(Np, Np)
                Tm = Tm + dot(Tm, Np)
                lev *= 2
            AV = dot(jnp.concatenate([Aak, Ark], axis=0), v)
            TW = dot(Tm, jnp.concatenate([At, AV[:L]], axis=1))
            W1 = jnp.where(he, TW[:, :LANES], W1)
            W2 = jnp.where(he, TW[:, LANES:], W2)
            Op = jnp.where(he, AV[L:], Op)
        S = sbd_ref[p]
        XS = dg(jnp.concatenate([W1, Rt], axis=0), S, NT)
        U = XS[:L] + W2
        O = XS[L:] + Op
        for e in range(2):
            he = (lane >= HEAD) if e else (lane < HEAD)
            O = O + jnp.where(he, dot(arbs[e], U), 0.0)
        upd = dg(jnp.concatenate([U, v], axis=0), jnp.concatenate([b * ratio, k2 * ratio], axis=0), TN)
        sbd_ref[p] = S * jnp.exp(lc_last) + jnp.where(same_head, upd, 0.0)
        mu = _seg_allsum(O, HEAD) * (1.0 / HEAD)
        d = O - mu
        var = _seg_allsum(d * d, HEAD) * (1.0 / HEAD)
        gn = d * lax.rsqrt(var + GN_EPS) * lng_ref[:, sl] + lnb_ref[:, sl]
        y_ref[:, sl] = (gn * g_ref[:, sl] + bg_ref[:, sl]).astype(y_ref.dtype)

    @pl.when(c == pl.num_programs(2) - 1)
    def _():
        so_ref[0] = sbd_ref[...]


def _rwkv_scan(prep, lnx_g, lnx_b, s0_bd, B, T):
    RW = prep[0].shape[1]
    L = SCAN_L
    assert T % L == 0
    nc = T // L
    PB = 1
    npair = RW // LANES
    blk = pl.BlockSpec((L, PB * LANES), lambda b, p, c: (b * nc + c, p))
    vec = pl.BlockSpec((1, PB * LANES), lambda b, p, c: (0, p))
    sblk = pl.BlockSpec((1, PB, LANES, LANES), lambda b, p, c: (b, p, 0, 0))
    y, s_out = pl.pallas_call(
        functools.partial(_scan_kernel, PB=PB), grid=(B, npair // PB, nc),
        in_specs=[blk] * 8 + [vec, vec, sblk],
        out_specs=[blk, sblk],
        out_shape=[jax.ShapeDtypeStruct((B * T, RW), BF16),
                   jax.ShapeDtypeStruct((B, npair, LANES, LANES), F32)],
        scratch_shapes=[pltpu.VMEM((PB, LANES, LANES), F32)],
        compiler_params=_cp(("parallel", "parallel", "arbitrary")), name="rwkv_scan",
    )(*prep, lnx_g.reshape(1, RW), lnx_b.reshape(1, RW), s0_bd)
    return y, s_out


def _to_blockdiag(s):
    B, H = s.shape[:2]
    s = s.reshape(B, H // 2, 2, HEAD, HEAD)
    z = jnp.zeros_like(s[:, :, 0])
    top = jnp.concatenate([s[:, :, 0], z], axis=-1)
    bot = jnp.concatenate([z, s[:, :, 1]], axis=-1)
    return jnp.concatenate([top, bot], axis=-2)


def _from_blockdiag(sbd):
    B, P = sbd.shape[:2]
    return jnp.stack([sbd[:, :, :HEAD, :HEAD], sbd[:, :, HEAD:, HEAD:]], axis=2).reshape(B, 2 * P, HEAD, HEAD)


def _qknorm_kernel(q_ref, k_ref, v_ref, c_ref, s1_ref, s2_ref, qg_ref, kg_ref,
                   qb_o, kf_o, kb_o, vf_o, vb_o):
    cos, s1, s2 = c_ref[...], s1_ref[...], s2_ref[...]

    def norm_rope(x, gain):
        ms = _seg_allsum(x * x, HEAD) * (1.0 / HEAD)
        y = x * lax.rsqrt(ms + RMS_EPS) * gain
        half = ROT_DIM // 2
        return y * cos + pltpu.roll(y, LANES - half, axis=1) * s1 + pltpu.roll(y, half, axis=1) * s2

    for c in range(0, q_ref.shape[1], LANES):
        cs = slice(c, c + LANES)
        q = norm_rope(q_ref[:, cs], qg_ref[...])
        k = norm_rope(k_ref[:, cs], kg_ref[...])
        qb_o[:, cs] = (q * (HEAD ** -0.5)).astype(qb_o.dtype)
        kf_o[:, cs] = k
        kb_o[:, cs] = k.astype(kb_o.dtype)
    v = v_ref[...]
    vf_o[...] = v
    vb_o[...] = v.astype(vb_o.dtype)


def _rope_tables(pos):
    half = ROT_DIM // 2
    inv_freq = jnp.power(jnp.float32(ROPE_THETA), -jnp.arange(0, ROT_DIM, 2, dtype=jnp.float32) / ROT_DIM)
    ang = pos.astype(jnp.float32)[:, None] * inv_freq[None, :]
    cos, sin = jnp.cos(ang), jnp.sin(ang)
    T = pos.shape[0]
    pad = jnp.zeros((T, HEAD - ROT_DIM), F32)
    z = jnp.zeros((T, half), F32)
    c = jnp.concatenate([cos, cos, pad + 1.0], axis=1)
    s1 = jnp.concatenate([-sin, z, pad], axis=1)
    s2 = jnp.concatenate([z, sin, pad], axis=1)
    tile = lambda t: jnp.concatenate([t, t], axis=1)
    return tile(c), tile(s1), tile(s2)


def _qknorm(proj, row_off, rows, q_off, DW, tables, q_norm_g, k_norm_g):
    tr = _pick(rows, 256, 16)
    cw = _pick(DW, SEG_ALIGN, LANES)
    assert row_off % tr == 0 and q_off % cw == 0 and DW % cw == 0
    r0 = row_off // tr
    c0 = q_off // cw
    nc = DW // cw
    gain = lambda g: jnp.concatenate([g, g]).reshape(1, LANES)
    seg = lambda s: pl.BlockSpec((tr, cw), lambda i, j: (r0 + i, c0 + s * nc + j))
    tab = pl.BlockSpec((tr, LANES), lambda i, j: (i, 0))
    vec = pl.BlockSpec((1, LANES), lambda i, j: (0, 0))
    out = pl.BlockSpec((tr, cw), lambda i, j: (i, j))
    sds = lambda dt: jax.ShapeDtypeStruct((rows, DW), dt)
    return pl.pallas_call(
        _qknorm_kernel, grid=(rows // tr, nc),
        in_specs=[seg(0), seg(1), seg(2), tab, tab, tab, vec, vec],
        out_specs=[out] * 5,
        out_shape=[sds(BF16), sds(F32), sds(BF16), sds(F32), sds(BF16)],
        compiler_params=_cp(("parallel", "parallel")), name="qk_norm_rope",
    )(proj, proj, proj, *tables, gain(q_norm_g), gain(k_norm_g))


def _lambda(lam_ref, lam_init):
    lam = lam_ref[...]
    return (jnp.exp(jnp.sum(lam[0:1] * lam[1:2], axis=1, keepdims=True))
            - jnp.exp(jnp.sum(lam[2:3] * lam[3:4], axis=1, keepdims=True)) + lam_init)


def _stack_components(q):
    lane = lax.broadcasted_iota(jnp.int32, q.shape, 1)
    zero = jnp.zeros_like(q)
    return jnp.concatenate([jnp.where(lane < HEAD, q, zero), jnp.where(lane >= HEAD, q, zero)], axis=0)


def _softmax_step(qq, k, v, m_ref, l_ref, acc_ref, mask):
    s = lax.dot_general(qq, k, NT, preferred_element_type=F32)
    if mask is not None:
        s = jnp.where(mask, s, NEG_INF)
    m_old = m_ref[...]
    m_new = jnp.maximum(m_old, jnp.max(s, axis=1, keepdims=True))
    alpha = jnp.exp(m_old - m_new)
    p = jnp.exp(s - m_new[:, 0:1])
    l_ref[...] = alpha * l_ref[...] + jnp.sum(p, axis=1, keepdims=True)
    acc_ref[...] = alpha * acc_ref[...] + jnp.dot(p.astype(BF16), v, preferred_element_type=F32)
    m_ref[...] = m_new


def _finish_head(t, m_ref, l_ref, acc_ref, lam, g, lam_init):
    o = acc_ref[...] / l_ref[...]
    o = o[:t] - lam * o[t:]
    ms = jnp.mean(o * o, axis=-1, keepdims=True)
    return o * lax.rsqrt(ms + RMS_EPS) * g * (1.0 - lam_init)


def _attn_prompt_kernel(lam_ref, g_ref, q_ref, k_ref, v_ref, o_ref, m_ref, l_ref, acc_ref, *, lam_init):
    qi = pl.program_id(2)
    tq = q_ref.shape[0]
    qq = _stack_components(q_ref[...])
    m_ref[...] = jnp.full(m_ref.shape, NEG_INF, F32)
    l_ref[...] = jnp.zeros(l_ref.shape, F32)
    acc_ref[...] = jnp.zeros(acc_ref.shape, F32)

    def body(j, carry):
        off = pl.multiple_of(j * tq, tq)
        _softmax_step(qq, k_ref[pl.ds(off, tq), :], v_ref[pl.ds(off, tq), :], m_ref, l_ref, acc_ref, None)
        return carry

    lax.fori_loop(0, qi, body, 0)
    r = lax.broadcasted_iota(jnp.int32, (2 * tq, tq), 0)
    cidx = lax.broadcasted_iota(jnp.int32, (2 * tq, tq), 1)
    r = jnp.where(r >= tq, r - tq, r)
    off = pl.multiple_of(qi * tq, tq)
    _softmax_step(qq, k_ref[pl.ds(off, tq), :], v_ref[pl.ds(off, tq), :], m_ref, l_ref, acc_ref,
                  (r // CHUNK) >= (cidx // CHUNK))
    o_ref[...] = _finish_head(tq, m_ref, l_ref, acc_ref, _lambda(lam_ref, lam_init), g_ref[...],
                              lam_init).astype(o_ref.dtype)


def _attn_prompt(qb, kb, vb, lam_vecs, subln_g, B, T, lam_init):
    DW = qb.shape[1]
    nh = DW // LANES
    tq = _pick(T, 256, CHUNK)
    nq = T // tq
    return pl.pallas_call(
        functools.partial(_attn_prompt_kernel, lam_init=lam_init), grid=(B, nh, nq),
        in_specs=[pl.BlockSpec(lam_vecs.shape, lambda b, h, i: (0, 0)),
                  pl.BlockSpec((1, LANES), lambda b, h, i: (0, 0)),
                  pl.BlockSpec((tq, LANES), lambda b, h, i: (b * nq + i, h)),
                  pl.BlockSpec((T, LANES), lambda b, h, i: (b, h)),
                  pl.BlockSpec((T, LANES), lambda b, h, i: (b, h))],
        out_specs=pl.BlockSpec((tq, LANES), lambda b, h, i: (b * nq + i, h)),
        out_shape=jax.ShapeDtypeStruct((B * T, DW), BF16),
        scratch_shapes=[pltpu.VMEM((2 * tq, LANES), F32)] * 3,
        compiler_params=_cp(("parallel", "parallel", "arbitrary")), name="attn_prompt",
    )(lam_vecs, subln_g.reshape(1, LANES), qb, kb, vb)


def _attn_sample_kernel(lam_ref, g_ref, q_ref, kc_ref, vc_ref, kn_ref, vn_ref, o_ref, m_ref, l_ref, acc_ref,
                        *, lam_init, past):
    j = pl.program_id(1)
    ts = q_ref.shape[0]
    nh = q_ref.shape[1] // LANES

    @pl.when(j == 0)
    def _():
        m_ref[...] = jnp.full(m_ref.shape, NEG_INF, F32)
        l_ref[...] = jnp.zeros(l_ref.shape, F32)
        acc_ref[...] = jnp.zeros(acc_ref.shape, F32)

    def heads(k_ref, v_ref, mask):
        for h in range(nh):
            cs = slice(h * LANES, (h + 1) * LANES)
            qq = _stack_components(q_ref[:, cs])
            _softmax_step(qq, k_ref[:, cs].astype(BF16), v_ref[:, cs].astype(BF16),
                          m_ref.at[h], l_ref.at[h], acc_ref.at[h], mask)

    heads(kc_ref, vc_ref, None)

    @pl.when(j == pl.num_programs(1) - 1)
    def _():
        r = lax.broadcasted_iota(jnp.int32, (2 * ts, ts), 0)
        cidx = lax.broadcasted_iota(jnp.int32, (2 * ts, ts), 1)
        r = jnp.where(r >= ts, r - ts, r)
        heads(kn_ref, vn_ref, ((past + r) // CHUNK) >= ((past + cidx) // CHUNK))
        lam = _lambda(lam_ref, lam_init)
        for h in range(nh):
            cs = slice(h * LANES, (h + 1) * LANES)
            o_ref[:, cs] = _finish_head(ts, m_ref.at[h], l_ref.at[h], acc_ref.at[h], lam, g_ref[...],
                                        lam_init).astype(o_ref.dtype)


def _attn_sample(qb, kb, vb, cache_k, cache_v, lam_vecs, subln_g, B, Ts, lam_init):
    DW = qb.shape[1]
    nh = DW // LANES
    past = cache_k.shape[0] // B
    tk = _pick(past, 512, 16)
    nkv = past // tk
    new = pl.BlockSpec((Ts, DW), lambda b, j: (b, 0))
    old = pl.BlockSpec((tk, DW), lambda b, j: (b * nkv + j, 0))
    return pl.pallas_call(
        functools.partial(_attn_sample_kernel, lam_init=lam_init, past=past), grid=(B, nkv),
        in_specs=[pl.BlockSpec(lam_vecs.shape, lambda b, j: (0, 0)),
                  pl.BlockSpec((1, LANES), lambda b, j: (0, 0)),
                  new, old, old, new, new],
        out_specs=new,
        out_shape=jax.ShapeDtypeStruct((B * Ts, DW), BF16),
        scratch_shapes=[pltpu.VMEM((nh, 2 * Ts, LANES), F32)] * 3,
        compiler_params=_cp(("parallel", "arbitrary")), name="attn_sample",
    )(lam_vecs, subln_g.reshape(1, LANES), qb, cache_k, cache_v, kb, vb)


def _layer(i, x_prompt, x_sample, pe_prompt, pe_sample, cache_k, cache_v, state_rwkv, state_shift, w):
    Bp, Tp, D = x_prompt.shape
    Bs, Ts, _ = x_sample.shape
    Mp, Ms = Bp * Tp, Bs * Ts
    past = cache_k.shape[1]
    RW = w["w0"].shape[-1]
    DW = w["w_branch_diff"].shape[0]
    SW = state_shift.shape[-1]
    SWp = _round_up(SW, SEG_ALIGN)
    lam_init = 0.8 - 0.6 * math.exp(-0.3 * i)

    w_in = w["w_in"]
    w_al = jnp.concatenate([w_in[:, :SW], jnp.zeros((D, SWp - SW), w_in.dtype), w_in[:, SW:]], axis=1).astype(BF16)
    q_off = SWp
    gate_off = SWp + 3 * DW

    x_all = jnp.concatenate([x_prompt.reshape(Mp, D), x_sample.reshape(Ms, D)], axis=0)
    h1 = _rmsnorm(x_all, w["norm1_g"])
    proj = _mm_fullk(_mm_kernel, h1, w_al, F32, tn_target=512, name="proj_in")

    pad_sw = lambda a: jnp.pad(a, ((0, 0), (0, 0), (0, SWp - SW)))
    mu = jnp.pad(w["tm_mu"], (0, SWp - SW))
    lam_vecs = jnp.stack([w["lam_q1"], w["lam_k1"], w["lam_q2"], w["lam_k2"]])
    lora = (mu, w["w0"], w["a0"], w["k_k"], w["k_a"], w["r_k"].reshape(-1), w["w_decay_up"], w["a_up"], w["g_up"])

    prep_p = _rwkv_prep(proj, 0, Bp, Tp, SWp, jnp.zeros((Bp, 1, SWp), F32), *lora)
    s0_p = jnp.zeros((Bp, RW // LANES, LANES, LANES), F32)
    yr_p, s_p = _rwkv_scan(prep_p, w["lnx_g"], w["lnx_b"], s0_p, Bp, Tp)
    tab_p = _rope_tables(jnp.arange(Tp, dtype=jnp.int32))
    tab_p = tuple(jnp.tile(t, (Bp, 1)) for t in tab_p)
    qb_p, kf_p, kb_p, vf_p, vb_p = _qknorm(proj, 0, Mp, q_off, DW, tab_p, w["q_norm_g"], w["k_norm_g"])
    yd_p = _attn_prompt(qb_p, kb_p, vb_p, lam_vecs, w["subln_g"], Bp, Tp, lam_init)

    prep_s = _rwkv_prep(proj, Mp, Bs, Ts, SWp, pad_sw(state_shift), *lora)
    Tpad = _round_up(Ts, SCAN_L)
    pad_t = lambda a: jnp.pad(a.reshape(Bs, Ts, RW), ((0, 0), (0, Tpad - Ts), (0, 0))).reshape(Bs * Tpad, RW)
    yr_s, s_s = _rwkv_scan([pad_t(a) for a in prep_s], w["lnx_g"], w["lnx_b"], _to_blockdiag(state_rwkv), Bs, Tpad)
    yr_s = yr_s.reshape(Bs, Tpad, RW)[:, :Ts].reshape(Ms, RW)
    tab_s = _rope_tables(past + jnp.arange(Ts, dtype=jnp.int32))
    tab_s = tuple(jnp.tile(t, (Bs, 1)) for t in tab_s)
    qb_s, kf_s, kb_s, vf_s, vb_s = _qknorm(proj, Mp, Ms, q_off, DW, tab_s, w["q_norm_g"], w["k_norm_g"])
    yd_s = _attn_sample(qb_s, kb_s, vb_s, cache_k.reshape(Bs * past, DW), cache_v.reshape(Bs * past, DW),
                        lam_vecs, w["subln_g"], Bs, Ts, lam_init)

    y_r = jnp.concatenate([yr_p, yr_s], axis=0)
    y_d = jnp.concatenate([yd_p, yd_s], axis=0)
    merged = _merge(y_r, y_d, w["w_branch_rwkv"], w["w_branch_diff"], proj, gate_off)
    x1 = _mm_fullk(_mm_res_kernel, merged, w["w_out"], F32, res=x_all, name="out_proj")
    h2 = _rmsnorm(x1, w["norm2_g"])
    u = _mm_fullk(_mm_relu2_kernel, h2, w["w_up"], BF16, name="ffn_up")
    x2 = _mm_res_kloop(u, w["w_down"], x1, name="ffn_down")
    h3 = _rmsnorm(x2, w["ple_norm_g"])
    y_p = _ple(h3, x2, pe_prompt.reshape(Mp, -1), w["w_ple_gate"], w["w_ple_proj"], 0, Mp)
    y_s = _ple(h3, x2, pe_sample.reshape(Ms, -1), w["w_ple_gate"], w["w_ple_proj"], Mp, Ms)

    nh_d = DW // LANES
    shift_p = proj[:Mp].reshape(Bp, Tp, -1)[:, Tp - 1:, :SW]
    shift_s = proj[Mp:].reshape(Bs, Ts, -1)[:, Ts - 1:, :SW]
    return (y_p.reshape(Bp, Tp, D), y_s.reshape(Bs, Ts, D),
            kf_p.reshape(Bp, Tp, nh_d, LANES), vf_p.reshape(Bp, Tp, nh_d, LANES), _from_blockdiag(s_p), shift_p,
            kf_s.reshape(Bs, Ts, nh_d, LANES), vf_s.reshape(Bs, Ts, nh_d, LANES), _from_blockdiag(s_s), shift_s)


def kernel(x_prompt, x_sample, p_prompt, p_sample, cache_k, cache_v, state_rwkv, state_shift, norm1_g, w_in, tm_mu, w_decay_up, w0, a_up, a0, g_up, k_k, k_a, r_k, lnx_g, lnx_b, q_norm_g, k_norm_g, lam_q1, lam_k1, lam_q2, lam_k2, subln_g, w_branch_rwkv, w_branch_diff, w_out, norm2_g, w_up, w_down, ple_norm_g, w_ple_gate, w_ple_proj):
    weights = dict(norm1_g=norm1_g, w_in=w_in, tm_mu=tm_mu, w_decay_up=w_decay_up, w0=w0, a_up=a_up, a0=a0,
                   g_up=g_up, k_k=k_k, k_a=k_a, r_k=r_k, lnx_g=lnx_g, lnx_b=lnx_b, q_norm_g=q_norm_g,
                   k_norm_g=k_norm_g, lam_q1=lam_q1, lam_k1=lam_k1, lam_q2=lam_q2, lam_k2=lam_k2,
                   subln_g=subln_g, w_branch_rwkv=w_branch_rwkv, w_branch_diff=w_branch_diff, w_out=w_out,
                   norm2_g=norm2_g, w_up=w_up, w_down=w_down, ple_norm_g=ple_norm_g, w_ple_gate=w_ple_gate,
                   w_ple_proj=w_ple_proj)
    depth = w_in.shape[0]
    hp, hs = x_prompt, x_sample
    outs = [[] for _ in range(8)]
    for i in range(depth):
        res = _layer(i, hp, hs, p_prompt[i], p_sample[i], cache_k[i], cache_v[i], state_rwkv[i], state_shift[i],
                     {name: val[i] for name, val in weights.items()})
        hp, hs = res[0], res[1]
        for acc, val in zip(outs, res[2:]):
            acc.append(val)
    return (hp, hs) + tuple(jnp.stack(acc, 0) for acc in outs)
```

```python
import functools
import math

import jax
import jax.numpy as jnp
from jax import lax
from jax.experimental import pallas as pl
from jax.experimental.pallas import tpu as pltpu

F32 = jnp.float32
BF16 = jnp.bfloat16

LANES = 128
HEAD = 64
RMS_EPS = 1e-6
GN_EPS = HEAD * 1e-5
NEG_INF = -1e30
CHUNK = 64
ROT_DIM = HEAD // 4
ROPE_THETA = 500000.0
SCAN_L = 128
SCAN_PAIRS = 8
ATTN_HEADS = 4
SEG_ALIGN = 512
VMEM_LIMIT = 56 * 1024 * 1024
NT = (((1,), (1,)), ((), ()))


def _round_up(n, m):
    return (n + m - 1) // m * m


def _pick(n, target, mult):
    best = None
    for d in range(mult, min(n, target) + 1, mult):
        if n % d == 0:
            best = d
    assert best is not None, (n, target, mult)
    return best


def _cp(sem):
    return pltpu.CompilerParams(dimension_semantics=sem, vmem_limit_bytes=VMEM_LIMIT)


def _bdot(a, b):
    return jnp.dot(a.astype(BF16), b.astype(BF16), preferred_element_type=F32)


def _bdot_nt(a, b):
    return lax.dot_general(a.astype(BF16), b.astype(BF16), NT, preferred_element_type=F32)


def _head_ones():
    r = lax.broadcasted_iota(jnp.int32, (2 * LANES, LANES), 0)
    c = lax.broadcasted_iota(jnp.int32, (2 * LANES, LANES), 1)
    r = jnp.where(r >= LANES, r - LANES, r)
    return jnp.where((r >= HEAD) == (c >= HEAD), 1.0, 0.0).astype(BF16)


def _head_allsum(x, ones2):
    hi = x.astype(BF16)
    lo = (x - hi.astype(F32)).astype(BF16)
    return jnp.dot(jnp.concatenate([hi, lo], axis=1), ones2, preferred_element_type=F32)


def _rmsnorm_kernel(x_ref, g_ref, o_ref):
    x = x_ref[...]
    ms = jnp.mean(x * x, axis=-1, keepdims=True)
    o_ref[...] = (x * lax.rsqrt(ms + RMS_EPS) * g_ref[...]).astype(o_ref.dtype)


def _rmsnorm(x, g):
    M, D = x.shape
    tr = _pick(M, 256, 16)
    return pl.pallas_call(
        _rmsnorm_kernel, grid=(M // tr,),
        in_specs=[pl.BlockSpec((tr, D), lambda i: (i, 0)), pl.BlockSpec((1, D), lambda i: (0, 0))],
        out_specs=pl.BlockSpec((tr, D), lambda i: (i, 0)),
        out_shape=jax.ShapeDtypeStruct((M, D), BF16),
        compiler_params=_cp(("parallel",)), name="rmsnorm")(x, g.reshape(1, D))


def _mm_kernel(a_ref, b_ref, o_ref):
    o_ref[...] = _bdot(a_ref[...], b_ref[...]).astype(o_ref.dtype)


def _mm_relu2_kernel(a_ref, b_ref, o_ref):
    p = jnp.maximum(_bdot(a_ref[...], b_ref[...]), 0.0)
    o_ref[...] = (p * p).astype(o_ref.dtype)


def _mm_res_kernel(a_ref, b_ref, r_ref, o_ref):
    o_ref[...] = r_ref[...] + _bdot(a_ref[...], b_ref[...])


def _mm_fullk(kernel, a, b, out_dtype, res=None, n_cols=None, name="mm"):
    M, K = a.shape
    N = n_cols or b.shape[1]
    tm = _pick(M, 1408, 16)
    tn = _pick(N, 256, LANES)
    in_specs = [pl.BlockSpec((tm, K), lambda i, j: (i, 0)), pl.BlockSpec((K, tn), lambda i, j: (0, j))]
    args = [a, b]
    if res is not None:
        in_specs.append(pl.BlockSpec((tm, tn), lambda i, j: (i, j)))
        args.append(res)
    return pl.pallas_call(
        kernel, grid=(M // tm, N // tn), in_specs=in_specs,
        out_specs=pl.BlockSpec((tm, tn), lambda i, j: (i, j)),
        out_shape=jax.ShapeDtypeStruct((M, N), out_dtype),
        compiler_params=_cp(("parallel", "parallel")), name=name)(*args)


def _mm_shift_kernel(a_ref, b0_ref, b1_ref, o_ref, *, shift):
    tn = o_ref.shape[1]
    w = jnp.concatenate([b0_ref[...], b1_ref[...]], axis=1)
    w = pltpu.roll(w, w.shape[1] - shift, axis=1)[:, :tn]
    o_ref[...] = _bdot(a_ref[...], w)


def _mm_from_col(a, b, col0, n_cols, name):
    M, K = a.shape
    base = col0 // LANES * LANES
    shift = col0 - base
    assert shift > 0
    tm = _pick(M, 1408, 16)
    tn = LANES * math.gcd(math.gcd(base // LANES, n_cols // LANES), 2)
    b0 = base // tn
    b1 = (base + tn) // LANES
    step = tn // LANES
    return pl.pallas_call(
        functools.partial(_mm_shift_kernel, shift=shift), grid=(M // tm, n_cols // tn),
        in_specs=[pl.BlockSpec((tm, K), lambda i, j: (i, 0)),
                  pl.BlockSpec((K, tn), lambda i, j: (0, b0 + j)),
                  pl.BlockSpec((K, LANES), lambda i, j: (0, b1 + j * step))],
        out_specs=pl.BlockSpec((tm, tn), lambda i, j: (i, j)),
        out_shape=jax.ShapeDtypeStruct((M, n_cols), F32),
        compiler_params=_cp(("parallel", "parallel")), name=name)(a, b, b)


def _mm_res_kloop_kernel(a_ref, b_ref, r_ref, o_ref):
    p = _bdot(a_ref[...], b_ref[...])

    @pl.when(pl.program_id(2) == 0)
    def _():
        o_ref[...] = r_ref[...] + p

    @pl.when(pl.program_id(2) > 0)
    def _():
        o_ref[...] += p


def _mm_res_kloop(a, b, res, name):
    M, K = a.shape
    N = b.shape[1]
    tm = _pick(M, 1408, 16)
    tn = _pick(N, 256, LANES)
    tk = _pick(K, 4096, LANES)
    return pl.pallas_call(
        _mm_res_kloop_kernel, grid=(M // tm, N // tn, K // tk),
        in_specs=[pl.BlockSpec((tm, tk), lambda i, j, k: (i, k)),
                  pl.BlockSpec((tk, tn), lambda i, j, k: (k, j)),
                  pl.BlockSpec((tm, tn), lambda i, j, k: (i, j))],
        out_specs=pl.BlockSpec((tm, tn), lambda i, j, k: (i, j)),
        out_shape=jax.ShapeDtypeStruct((M, N), F32),
        compiler_params=_cp(("parallel", "parallel", "arbitrary")), name=name)(a, b, res)


def _merge_kernel(yr_ref, yd_ref, wr_ref, wd_ref, gr_ref, gd_ref, o_ref):
    pr = _bdot(yr_ref[...], wr_ref[...])
    pd = _bdot(yd_ref[...], wd_ref[...])
    o_ref[...] = (jax.nn.sigmoid(gr_ref[...]) * pr + jax.nn.sigmoid(gd_ref[...]) * pd).astype(o_ref.dtype)


def _merge(y_r, y_d, w_r, w_d, proj, gate_off):
    M, Kr = y_r.shape
    Kd = y_d.shape[1]
    D = w_r.shape[1]
    tm = _pick(M, 1408, 16)
    tn = _pick(D, 256, LANES)
    assert gate_off % tn == 0
    g0 = gate_off // tn
    g1 = (gate_off + D) // tn
    return pl.pallas_call(
        _merge_kernel, grid=(M // tm, D // tn),
        in_specs=[pl.BlockSpec((tm, Kr), lambda i, j: (i, 0)),
                  pl.BlockSpec((tm, Kd), lambda i, j: (i, 0)),
                  pl.BlockSpec((Kr, tn), lambda i, j: (0, j)),
                  pl.BlockSpec((Kd, tn), lambda i, j: (0, j)),
                  pl.BlockSpec((tm, tn), lambda i, j: (i, g0 + j)),
                  pl.BlockSpec((tm, tn), lambda i, j: (i, g1 + j))],
        out_specs=pl.BlockSpec((tm, tn), lambda i, j: (i, j)),
        out_shape=jax.ShapeDtypeStruct((M, D), BF16),
        compiler_params=_cp(("parallel", "parallel")), name="branch_merge")(y_r, y_d, w_r, w_d, proj, proj)


def _ple_kernel(h_ref, wg_ref, pe_ref, wp_ref, x_ref, o_ref):
    gate = jax.nn.sigmoid(_bdot(h_ref[...], wg_ref[...]))
    o_ref[...] = x_ref[...] + gate * _bdot(pe_ref[...], wp_ref[...])


def _ple(h, x, pe, w_gate, w_proj, row_off, rows):
    D = x.shape[1]
    P = pe.shape[1]
    tm = _pick(rows, 1024, 16)
    tn = _pick(D, 256, LANES)
    assert row_off % tm == 0
    r0 = row_off // tm
    return pl.pallas_call(
        _ple_kernel, grid=(rows // tm, D // tn),
        in_specs=[pl.BlockSpec((tm, D), lambda i, j: (r0 + i, 0)),
                  pl.BlockSpec((D, tn), lambda i, j: (0, j)),
                  pl.BlockSpec((tm, P), lambda i, j: (i, 0)),
                  pl.BlockSpec((P, tn), lambda i, j: (0, j)),
                  pl.BlockSpec((tm, tn), lambda i, j: (r0 + i, j))],
        out_specs=pl.BlockSpec((tm, tn), lambda i, j: (i, j)),
        out_shape=jax.ShapeDtypeStruct((rows, D), F32),
        compiler_params=_cp(("parallel", "parallel")), name="ple")(h, w_gate, pe, w_proj, x)


def _head_allsum_wide(x, ones2):
    n = x.shape[1] // LANES
    rows = x.shape[0]
    s = _head_allsum(jnp.concatenate([x[:, i * LANES:(i + 1) * LANES] for i in range(n)], axis=0), ones2)
    return jnp.concatenate([s[i * rows:(i + 1) * rows] for i in range(n)], axis=1)


def _rwkv_kernel(pr_ref, pk_ref, pv_ref, pw_ref, spr_ref, spk_ref, spv_ref, spw_ref,
                 mur_ref, muk_ref, muv_ref, muw_ref, w0_ref, a0_ref, kk_ref, ka_ref, rk_ref,
                 wd_ref, wa_ref, wg_ref, lng_ref, lnb_ref, s0_ref,
                 y_ref, so_ref, sbd_ref, prev_ref, *, PB, L, win_w, win_a, win_g):
    c = pl.program_id(2)
    Tb = pr_ref.shape[0]
    W = PB * LANES
    lane = lax.broadcasted_iota(jnp.int32, (L, LANES), 1)
    lane2 = lax.broadcasted_iota(jnp.int32, (2 * L, LANES), 1)
    row = lax.broadcasted_iota(jnp.int32, (L, L), 0)
    col = lax.broadcasted_iota(jnp.int32, (L, L), 1)
    incl = row >= col
    strict = row > col
    tri = jnp.where(incl, 1.0, 0.0).astype(BF16)
    eye = jnp.where(row == col, 1.0, 0.0)
    srow = lax.broadcasted_iota(jnp.int32, (LANES, LANES), 0)
    scol = lax.broadcasted_iota(jnp.int32, (LANES, LANES), 1)
    same_head = (srow >= HEAD) == (scol >= HEAD)
    ones2 = _head_ones()

    @pl.when(c == 0)
    def _():
        sbd_ref[...] = s0_ref[0]
        prev_ref[:, 0:W] = spr_ref[0]
        prev_ref[:, W:2 * W] = spk_ref[0]
        prev_ref[:, 2 * W:3 * W] = spv_ref[0]
        prev_ref[:, 3 * W:] = spw_ref[0]

    first_row = lax.broadcasted_iota(jnp.int32, (Tb, 1), 0) == 0

    def shifted(x_ref, mu_ref, off):
        x = x_ref[...]
        prev = prev_ref[:, off:off + x.shape[1]]
        prev_ref[:, off:off + x.shape[1]] = x[Tb - 1:Tb, :]
        sh = jnp.where(first_row, prev, pltpu.roll(x, 1, axis=0))
        return x + (sh - x) * mu_ref[...]

    r = shifted(pr_ref, mur_ref, 0)
    k = shifted(pk_ref, muk_ref, W)
    v = shifted(pv_ref, muv_ref, 2 * W)
    lo = shifted(pw_ref, muw_ref, 3 * W)
    z = w0_ref[...] + _bdot(jnp.tanh(lo[:, win_w[0]:win_w[1]]), wd_ref[...])
    lw = -math.exp(-0.5) * jax.nn.sigmoid(z)
    alr = jax.nn.sigmoid(a0_ref[...] + _bdot(lo[:, win_a[0]:win_a[1]], wa_ref[...]))
    gate = _bdot(jax.nn.sigmoid(lo[:, win_g[0]:win_g[1]]), wg_ref[...])
    kk = k * kk_ref[...]
    k2 = k * (1.0 + (alr - 1.0) * ka_ref[...])
    kkn = kk * lax.rsqrt(_head_allsum_wide(kk * kk, ones2) + 1e-12)
    bonus_gate = _head_allsum_wide(r * k2 * rk_ref[...], ones2) * v * gate
    if Tb < L:
        pad = lambda x: jnp.concatenate([x, jnp.zeros((L - Tb, W), F32)], axis=0)
        r, k2, v, lw, kkn, alr = pad(r), pad(k2), pad(v), pad(lw), pad(kkn), pad(alr)

    pairs = range(PB)
    heads = [(p, e) for p in pairs for e in range(2)]
    cols = [slice(p * LANES, (p + 1) * LANES) for p in pairs]
    bmm = lambda x, y: lax.dot_general(x.astype(BF16), y.astype(BF16), (((2,), (1,)), ((0,), (0,))),
                                       preferred_element_type=F32)
    bmm_nt = lambda x, y: lax.dot_general(x.astype(BF16), y.astype(BF16), (((2,), (2,)), ((0,), (0,))),
                                          preferred_element_type=F32)
    rows_cat = lambda xs: jnp.concatenate(xs, axis=0)

    w1 = lw.astype(BF16)
    res1 = lw - w1.astype(F32)
    w2 = res1.astype(BF16)
    w3 = (res1 - w2.astype(F32)).astype(BF16)
    lc3 = jnp.dot(tri, jnp.concatenate([w1, w2, w3], axis=1), preferred_element_type=F32)
    lc = lc3[:, :W] + lc3[:, W:2 * W] + lc3[:, 2 * W:]
    lc_last = lc[L - 1:L, :]
    g_in, g_prev, g_inv = jnp.exp(lc), jnp.exp(lc - lw), jnp.exp(-lc)
    ratio = jnp.exp(lc_last - lc)
    g_last = jnp.exp(lc_last)
    b = kkn * alr
    At, Rt, Bt, Kt = -kkn * g_prev, r * g_in, b * g_inv, k2 * g_inv
    Bh, Kh = b * ratio, k2 * ratio

    he = lane >= HEAD
    own = [lane2 < HEAD, lane2 >= HEAD]
    X = [rows_cat([At[:, cs], Rt[:, cs]]) for cs in cols]
    Y = [rows_cat([Bt[:, cs], Kt[:, cs]]) for cs in cols]
    G = bmm_nt(jnp.stack([jnp.where(own[e], X[p], 0.0) for p, e in heads]),
               jnp.stack([Y[p] for p, e in heads]))
    Nab = jnp.where(strict, G[:, :L, :L], 0.0)
    Aak = jnp.where(strict, G[:, :L, L:], 0.0)
    Arb = jnp.where(incl, G[:, L:, :L], 0.0)
    Ark = jnp.where(incl, G[:, L:, L:], 0.0)
    Tm = eye + Nab
    Np = Nab
    lev = 1
    while lev * 2 < L:
        Np = bmm(Np, Np)
        Tm = Tm + bmm(Tm, Np)
        lev *= 2
    AV = bmm(jnp.concatenate([Aak, Ark], axis=1), jnp.stack([v[:, cols[p]] for p, e in heads]))
    TW = bmm(Tm, jnp.concatenate([jnp.stack([At[:, cols[p]] for p, e in heads]), AV[:, :L]], axis=2))
    pick = lambda x, p: jnp.where(he, x[2 * p + 1], x[2 * p])
    W1 = [pick(TW[:, :, :LANES], p) for p in pairs]
    W2 = [pick(TW[:, :, LANES:], p) for p in pairs]
    Op = [pick(AV[:, L:], p) for p in pairs]
    S = sbd_ref[...]
    XS = bmm_nt(jnp.stack([rows_cat([W1[p], Rt[:, cols[p]]]) for p in pairs]), S)
    U = [XS[p, :L] + W2[p] for p in pairs]
    AU = bmm(Arb, jnp.stack([U[p] for p, e in heads]))
    O = [XS[p, L:] + Op[p] + pick(AU, p) for p in pairs]
    upd = bmm(jnp.stack([rows_cat([U[p], v[:, cols[p]]]).T for p in pairs]),
              jnp.stack([rows_cat([Bh[:, cols[p]], Kh[:, cols[p]]]) for p in pairs]))
    sbd_ref[...] = S * jnp.stack([g_last[:, cols[p]] for p in pairs]) + jnp.where(same_head, upd, 0.0)
    Oc = rows_cat(O)
    mu = _head_allsum(Oc, ones2) * (1.0 / HEAD)
    d = Oc - mu
    var = _head_allsum(d * d, ones2) * (1.0 / HEAD)
    nrm = d * lax.rsqrt(var + GN_EPS)
    for p in pairs:
        gn = nrm[p * L:p * L + Tb] * lng_ref[:, cols[p]] + lnb_ref[:, cols[p]]
        y_ref[:, cols[p]] = (gn * gate[:, cols[p]] + bonus_gate[:, cols[p]]).astype(y_ref.dtype)

    @pl.when(c == pl.num_programs(2) - 1)
    def _():
        so_ref[0] = sbd_ref[...]


def _lora_window(start, width, w, base):
    s0 = start // LANES * LANES
    s1 = _round_up(start + width, LANES)
    wp = jnp.zeros((s1 - s0, w.shape[1]), w.dtype).at[start - s0:start - s0 + width].set(w)
    return (s0 - base, s1 - base), wp


def _rwkv(proj, row_off, B, T, SWp, shift_prev, s0_bd, mu, w0, a0, k_k, k_a, r_k, w_decay_up, a_up, g_up,
          lnx_g, lnx_b):
    RW = w0.shape[-1]
    L = SCAN_L
    Tb = min(T, L)
    assert T % Tb == 0 and row_off % Tb == 0
    nc = T // Tb
    r0 = row_off // Tb
    npair = RW // LANES
    PB = math.gcd(npair, SCAN_PAIRS)
    W = PB * LANES
    ng = RW // W
    LW = SWp - 3 * RW
    assert LW % LANES == 0 and (3 * RW) % LW == 0
    lb = 3 * RW // LW
    DL, AL, GL = w_decay_up.shape[0], a_up.shape[0], g_up.shape[0]
    win_w, wd = _lora_window(3 * RW, DL, w_decay_up, 3 * RW)
    win_a, wa = _lora_window(3 * RW + DL, AL, a_up, 3 * RW)
    win_g, wg = _lora_window(3 * RW + DL + AL, GL, g_up, 3 * RW)
    row = lambda x: x.reshape(1, -1)
    seg = lambda s: pl.BlockSpec((Tb, W), lambda b, g, c: (r0 + b * nc + c, s * ng + g))
    seg_prev = lambda s: pl.BlockSpec((1, 1, W), lambda b, g, c: (b, 0, s * ng + g))
    seg_mu = lambda s: pl.BlockSpec((1, W), lambda b, g, c: (0, s * ng + g))
    vec = pl.BlockSpec((1, W), lambda b, g, c: (0, g))
    wblk = lambda w: pl.BlockSpec((w.shape[0], W), lambda b, g, c: (0, g))
    sblk = pl.BlockSpec((1, PB, LANES, LANES), lambda b, g, c: (b, g, 0, 0))
    mu2 = row(mu)
    y, s_out = pl.pallas_call(
        functools.partial(_rwkv_kernel, PB=PB, L=L, win_w=win_w, win_a=win_a, win_g=win_g),
        grid=(B, ng, nc),
        in_specs=[seg(0), seg(1), seg(2), pl.BlockSpec((Tb, LW), lambda b, g, c: (r0 + b * nc + c, lb)),
                  seg_prev(0), seg_prev(1), seg_prev(2), pl.BlockSpec((1, 1, LW), lambda b, g, c: (b, 0, lb)),
                  seg_mu(0), seg_mu(1), seg_mu(2), pl.BlockSpec((1, LW), lambda b, g, c: (0, lb)),
                  vec, vec, vec, vec, vec, wblk(wd), wblk(wa), wblk(wg), vec, vec, sblk],
        out_specs=[pl.BlockSpec((Tb, W), lambda b, g, c: (b * nc + c, g)), sblk],
        out_shape=[jax.ShapeDtypeStruct((B * T, RW), BF16),
                   jax.ShapeDtypeStruct((B, npair, LANES, LANES), F32)],
        scratch_shapes=[pltpu.VMEM((PB, LANES, LANES), F32), pltpu.VMEM((1, 3 * W + LW), F32)],
        compiler_params=_cp(("parallel", "parallel", "arbitrary")), name="rwkv",
    )(proj, proj, proj, proj, shift_prev, shift_prev, shift_prev, shift_prev, mu2, mu2, mu2, mu2,
      row(w0), row(a0), row(k_k), row(k_a), row(r_k), wd, wa, wg, row(lnx_g), row(lnx_b), s0_bd)
    return y, s_out


def _to_blockdiag(s):
    B, H = s.shape[:2]
    s = s.reshape(B, H // 2, 2, HEAD, HEAD)
    z = jnp.zeros_like(s[:, :, 0])
    top = jnp.concatenate([s[:, :, 0], z], axis=-1)
    bot = jnp.concatenate([z, s[:, :, 1]], axis=-1)
    return jnp.concatenate([top, bot], axis=-2)


def _from_blockdiag(sbd):
    B, P = sbd.shape[:2]
    return jnp.stack([sbd[:, :, :HEAD, :HEAD], sbd[:, :, HEAD:, HEAD:]], axis=2).reshape(B, 2 * P, HEAD, HEAD)


def _qknorm_kernel(q_ref, k_ref, v_ref, c_ref, s1_ref, s2_ref, qg_ref, kg_ref,
                   qb_o, kf_o, kb_o, vf_o, vb_o):
    cos, s1, s2 = c_ref[...], s1_ref[...], s2_ref[...]
    ones2 = _head_ones()

    def norm_rope(x, gain):
        ms = _head_allsum(x * x, ones2) * (1.0 / HEAD)
        y = x * lax.rsqrt(ms + RMS_EPS) * gain
        half = ROT_DIM // 2
        return y * cos + pltpu.roll(y, LANES - half, axis=1) * s1 + pltpu.roll(y, half, axis=1) * s2

    for c in range(0, q_ref.shape[1], LANES):
        cs = slice(c, c + LANES)
        q = norm_rope(q_ref[:, cs], qg_ref[...])
        k = norm_rope(k_ref[:, cs], kg_ref[...])
        qb_o[:, cs] = (q * (HEAD ** -0.5)).astype(qb_o.dtype)
        kf_o[:, cs] = k
        kb_o[:, cs] = k.astype(kb_o.dtype)
    v = v_ref[...]
    vf_o[...] = v
    vb_o[...] = v.astype(vb_o.dtype)


def _rope_tables(pos):
    half = ROT_DIM // 2
    inv_freq = jnp.power(jnp.float32(ROPE_THETA), -jnp.arange(0, ROT_DIM, 2, dtype=jnp.float32) / ROT_DIM)
    ang = pos.astype(jnp.float32)[:, None] * inv_freq[None, :]
    cos, sin = jnp.cos(ang), jnp.sin(ang)
    T = pos.shape[0]
    pad = jnp.zeros((T, HEAD - ROT_DIM), F32)
    z = jnp.zeros((T, half), F32)
    c = jnp.concatenate([cos, cos, pad + 1.0], axis=1)
    s1 = jnp.concatenate([-sin, z, pad], axis=1)
    s2 = jnp.concatenate([z, sin, pad], axis=1)
    tile = lambda t: jnp.concatenate([t, t], axis=1)
    return tile(c), tile(s1), tile(s2)


def _qknorm(proj, row_off, rows, DW, tables, q_norm_g, k_norm_g):
    tr = _pick(rows, 256, 16)
    cw = _pick(DW, 512, LANES)
    assert row_off % tr == 0
    r0 = row_off // tr
    nc = DW // cw
    gain = lambda g: jnp.concatenate([g, g]).reshape(1, LANES)
    seg = lambda s: pl.BlockSpec((tr, cw), lambda i, j: (r0 + i, s * nc + j))
    tab = pl.BlockSpec((tr, LANES), lambda i, j: (i, 0))
    vec = pl.BlockSpec((1, LANES), lambda i, j: (0, 0))
    out = pl.BlockSpec((tr, cw), lambda i, j: (i, j))
    sds = lambda dt: jax.ShapeDtypeStruct((rows, DW), dt)
    return pl.pallas_call(
        _qknorm_kernel, grid=(rows // tr, nc),
        in_specs=[seg(0), seg(1), seg(2), tab, tab, tab, vec, vec],
        out_specs=[out] * 5,
        out_shape=[sds(BF16), sds(F32), sds(BF16), sds(F32), sds(BF16)],
        compiler_params=_cp(("parallel", "parallel")), name="qk_norm_rope",
    )(proj, proj, proj, *tables, gain(q_norm_g), gain(k_norm_g))


def _lambda(lam_ref, lam_init):
    lam = lam_ref[...]
    return (jnp.exp(jnp.sum(lam[0:1] * lam[1:2], axis=1, keepdims=True))
            - jnp.exp(jnp.sum(lam[2:3] * lam[3:4], axis=1, keepdims=True)) + lam_init)


def _stack_components(q):
    lane = lax.broadcasted_iota(jnp.int32, q.shape, 1)
    zero = jnp.zeros_like(q)
    return jnp.concatenate([jnp.where(lane < HEAD, q, zero), jnp.where(lane >= HEAD, q, zero)], axis=0)


def _softmax_step(qq, k, v, m_ref, l_ref, acc_ref, mask):
    s = lax.dot_general(qq, k, (((2,), (2,)), ((0,), (0,))), preferred_element_type=F32)
    if mask is not None:
        s = jnp.where(mask, s, NEG_INF)
    m_old = m_ref[...]
    m_new = jnp.maximum(m_old, jnp.max(s, axis=2, keepdims=True))
    alpha = jnp.exp(m_old - m_new)
    tk = s.shape[2]
    p = jnp.exp(s - jnp.concatenate([m_new] * pl.cdiv(tk, LANES), axis=2)[:, :, :tk])
    v1 = jnp.concatenate([v, jnp.ones(v.shape, v.dtype)], axis=2)
    pv = lax.dot_general(p.astype(BF16), v1, (((2,), (1,)), ((0,), (0,))), preferred_element_type=F32)
    l_ref[...] = alpha * l_ref[...] + pv[:, :, LANES:]
    acc_ref[...] = alpha * acc_ref[...] + pv[:, :, :LANES]
    m_ref[...] = m_new


def _finish_head(t, m_ref, l_ref, acc_ref, lam, g, lam_init):
    o = acc_ref[...] / l_ref[...]
    o = o[:t] - lam * o[t:]
    ms = jnp.mean(o * o, axis=-1, keepdims=True)
    return o * lax.rsqrt(ms + RMS_EPS) * g * (1.0 - lam_init)


def _attn_prompt_kernel(lam_ref, g_ref, q_ref, k_ref, v_ref, o_ref, m_ref, l_ref, acc_ref, *, lam_init):
    qi = pl.program_id(2)
    tq = q_ref.shape[0]
    nh = q_ref.shape[1] // LANES
    cols = [slice(h * LANES, (h + 1) * LANES) for h in range(nh)]
    qq = jnp.stack([_stack_components(q_ref[:, cs]) for cs in cols])
    m_ref[...] = jnp.full(m_ref.shape, NEG_INF, F32)
    l_ref[...] = jnp.zeros(l_ref.shape, F32)
    acc_ref[...] = jnp.zeros(acc_ref.shape, F32)

    def block(j, mask):
        rows = pl.ds(pl.multiple_of(j * tq, tq), tq)
        _softmax_step(qq, jnp.stack([k_ref[rows, cs] for cs in cols]), jnp.stack([v_ref[rows, cs] for cs in cols]),
                      m_ref, l_ref, acc_ref, mask)

    def body(j, carry):
        block(j, None)
        return carry

    lax.fori_loop(0, qi, body, 0)
    r = lax.broadcasted_iota(jnp.int32, (2 * tq, tq), 0)
    cidx = lax.broadcasted_iota(jnp.int32, (2 * tq, tq), 1)
    r = jnp.where(r >= tq, r - tq, r)
    block(qi, (r // CHUNK) >= (cidx // CHUNK))
    lam = _lambda(lam_ref, lam_init)
    for h, cs in enumerate(cols):
        o_ref[:, cs] = _finish_head(tq, m_ref.at[h], l_ref.at[h], acc_ref.at[h], lam, g_ref[...],
                                    lam_init).astype(o_ref.dtype)


def _attn_prompt(qb, kb, vb, lam_vecs, subln_g, B, T, lam_init):
    DW = qb.shape[1]
    nh = DW // LANES
    hg = math.gcd(nh, ATTN_HEADS)
    tq = _pick(T, 256, CHUNK)
    nq = T // tq
    return pl.pallas_call(
        functools.partial(_attn_prompt_kernel, lam_init=lam_init), grid=(B, nh // hg, nq),
        in_specs=[pl.BlockSpec(lam_vecs.shape, lambda b, h, i: (0, 0)),
                  pl.BlockSpec((1, LANES), lambda b, h, i: (0, 0)),
                  pl.BlockSpec((tq, hg * LANES), lambda b, h, i: (b * nq + i, h)),
                  pl.BlockSpec((T, hg * LANES), lambda b, h, i: (b, h)),
                  pl.BlockSpec((T, hg * LANES), lambda b, h, i: (b, h))],
        out_specs=pl.BlockSpec((tq, hg * LANES), lambda b, h, i: (b * nq + i, h)),
        out_shape=jax.ShapeDtypeStruct((B * T, DW), BF16),
        scratch_shapes=[pltpu.VMEM((hg, 2 * tq, LANES), F32)] * 3,
        compiler_params=_cp(("parallel", "parallel", "arbitrary")), name="attn_prompt",
    )(lam_vecs, subln_g.reshape(1, LANES), qb, kb, vb)


def _attn_sample_kernel(lam_ref, g_ref, q_ref, kc_ref, vc_ref, kn_ref, vn_ref, o_ref, m_ref, l_ref, acc_ref,
                        *, lam_init, past):
    j = pl.program_id(1)
    ts = q_ref.shape[0]
    nh = q_ref.shape[1] // LANES

    @pl.when(j == 0)
    def _():
        m_ref[...] = jnp.full(m_ref.shape, NEG_INF, F32)
        l_ref[...] = jnp.zeros(l_ref.shape, F32)
        acc_ref[...] = jnp.zeros(acc_ref.shape, F32)

    cols = [slice(h * LANES, (h + 1) * LANES) for h in range(nh)]
    tk = kc_ref.shape[0] // nh

    def heads(head_k, head_v, mask):
        qq = jnp.stack([_stack_components(q_ref[:, cs]) for cs in cols])
        _softmax_step(qq, jnp.stack([head_k(h).astype(BF16) for h in range(nh)]),
                      jnp.stack([head_v(h).astype(BF16) for h in range(nh)]), m_ref, l_ref, acc_ref, mask)

    heads(lambda h: kc_ref[pl.ds(h, tk, stride=nh), :], lambda h: vc_ref[pl.ds(h, tk, stride=nh), :], None)

    @pl.when(j == pl.num_programs(1) - 1)
    def _():
        r = lax.broadcasted_iota(jnp.int32, (2 * ts, ts), 0)
        cidx = lax.broadcasted_iota(jnp.int32, (2 * ts, ts), 1)
        r = jnp.where(r >= ts, r - ts, r)
        heads(lambda h: kn_ref[:, cols[h]], lambda h: vn_ref[:, cols[h]],
              ((past + r) // CHUNK) >= ((past + cidx) // CHUNK))
        lam = _lambda(lam_ref, lam_init)
        for h in range(nh):
            cs = slice(h * LANES, (h + 1) * LANES)
            o_ref[:, cs] = _finish_head(ts, m_ref.at[h], l_ref.at[h], acc_ref.at[h], lam, g_ref[...],
                                        lam_init).astype(o_ref.dtype)


def _attn_sample(qb, kb, vb, cache_k, cache_v, lam_vecs, subln_g, B, Ts, lam_init):
    DW = qb.shape[1]
    nh = DW // LANES
    past = cache_k.shape[0] // (B * nh)
    tk = _pick(past, 512, 16)
    nkv = past // tk
    new = pl.BlockSpec((Ts, DW), lambda b, j: (b, 0))
    old = pl.BlockSpec((tk * nh, LANES), lambda b, j: (b * nkv + j, 0))
    return pl.pallas_call(
        functools.partial(_attn_sample_kernel, lam_init=lam_init, past=past), grid=(B, nkv),
        in_specs=[pl.BlockSpec(lam_vecs.shape, lambda b, j: (0, 0)),
                  pl.BlockSpec((1, LANES), lambda b, j: (0, 0)),
                  new, old, old, new, new],
        out_specs=new,
        out_shape=jax.ShapeDtypeStruct((B * Ts, DW), BF16),
        scratch_shapes=[pltpu.VMEM((nh, 2 * Ts, LANES), F32)] * 3,
        compiler_params=_cp(("parallel", "arbitrary")), name="attn_sample",
    )(lam_vecs, subln_g.reshape(1, LANES), qb, cache_k, cache_v, kb, vb)


def _layer(i, x_prompt, x_sample, pe_prompt, pe_sample, cache_k, cache_v, state_rwkv, state_shift, w):
    Bp, Tp, D = x_prompt.shape
    Bs, Ts, _ = x_sample.shape
    Mp, Ms = Bp * Tp, Bs * Ts
    past = cache_k.shape[1]
    RW = w["w0"].shape[-1]
    DW = w["w_branch_diff"].shape[0]
    SW = state_shift.shape[-1]
    SWp = _round_up(SW, SEG_ALIGN)
    w_in = w["w_in"]
    lam_init = 0.8 - 0.6 * math.exp(-0.3 * i)

    x_all = jnp.concatenate([x_prompt.reshape(Mp, D), x_sample.reshape(Ms, D)], axis=0)
    h1 = _rmsnorm(x_all, w["norm1_g"])
    proj_r = _mm_fullk(_mm_kernel, h1, w_in, F32, n_cols=SWp, name="proj_rwkv")
    proj_a = _mm_from_col(h1, w_in, SW, w_in.shape[1] - SW, name="proj_attn")

    pad_sw = lambda a: jnp.pad(a, ((0, 0), (0, 0), (0, SWp - SW)))
    mu = jnp.pad(w["tm_mu"], (0, SWp - SW))
    lam_vecs = jnp.stack([w["lam_q1"], w["lam_k1"], w["lam_q2"], w["lam_k2"]])
    rwkv_w = (mu, w["w0"], w["a0"], w["k_k"], w["k_a"], w["r_k"].reshape(-1), w["w_decay_up"], w["a_up"], w["g_up"],
              w["lnx_g"], w["lnx_b"])

    s0_p = jnp.zeros((Bp, RW // LANES, LANES, LANES), F32)
    yr_p, s_p = _rwkv(proj_r, 0, Bp, Tp, SWp, jnp.zeros((Bp, 1, SWp), F32), s0_p, *rwkv_w)
    tab_p = _rope_tables(jnp.arange(Tp, dtype=jnp.int32))
    tab_p = tuple(jnp.tile(t, (Bp, 1)) for t in tab_p)
    qb_p, kf_p, kb_p, vf_p, vb_p = _qknorm(proj_a, 0, Mp, DW, tab_p, w["q_norm_g"], w["k_norm_g"])
    yd_p = _attn_prompt(qb_p, kb_p, vb_p, lam_vecs, w["subln_g"], Bp, Tp, lam_init)

    yr_s, s_s = _rwkv(proj_r, Mp, Bs, Ts, SWp, pad_sw(state_shift), _to_blockdiag(state_rwkv), *rwkv_w)
    tab_s = _rope_tables(past + jnp.arange(Ts, dtype=jnp.int32))
    tab_s = tuple(jnp.tile(t, (Bs, 1)) for t in tab_s)
    qb_s, kf_s, kb_s, vf_s, vb_s = _qknorm(proj_a, Mp, Ms, DW, tab_s, w["q_norm_g"], w["k_norm_g"])
    yd_s = _attn_sample(qb_s, kb_s, vb_s, cache_k.reshape(-1, LANES), cache_v.reshape(-1, LANES),
                        lam_vecs, w["subln_g"], Bs, Ts, lam_init)

    y_r = jnp.concatenate([yr_p, yr_s], axis=0)
    y_d = jnp.concatenate([yd_p, yd_s], axis=0)
    merged = _merge(y_r, y_d, w["w_branch_rwkv"], w["w_branch_diff"], proj_a, 3 * DW)
    x1 = _mm_fullk(_mm_res_kernel, merged, w["w_out"], F32, res=x_all, name="out_proj")
    h2 = _rmsnorm(x1, w["norm2_g"])
    u = _mm_fullk(_mm_relu2_kernel, h2, w["w_up"], BF16, name="ffn_up")
    x2 = _mm_res_kloop(u, w["w_down"], x1, name="ffn_down")
    h3 = _rmsnorm(x2, w["ple_norm_g"])
    y_p = _ple(h3, x2, pe_prompt.reshape(Mp, -1), w["w_ple_gate"], w["w_ple_proj"], 0, Mp)
    y_s = _ple(h3, x2, pe_sample.reshape(Ms, -1), w["w_ple_gate"], w["w_ple_proj"], Mp, Ms)

    nh_d = DW // LANES
    shift_p = proj_r[Tp - 1:Mp:Tp, :SW].reshape(Bp, 1, SW)
    shift_s = proj_r[Mp + Ts - 1::Ts, :SW].reshape(Bs, 1, SW)
    return (y_p.reshape(Bp, Tp, D), y_s.reshape(Bs, Ts, D),
            kf_p.reshape(Bp, Tp, nh_d, LANES), vf_p.reshape(Bp, Tp, nh_d, LANES), _from_blockdiag(s_p), shift_p,
            kf_s.reshape(Bs, Ts, nh_d, LANES), vf_s.reshape(Bs, Ts, nh_d, LANES), _from_blockdiag(s_s), shift_s)


def kernel(x_prompt, x_sample, p_prompt, p_sample, cache_k, cache_v, state_rwkv, state_shift, norm1_g, w_in, tm_mu, w_decay_up, w0, a_up, a0, g_up, k_k, k_a, r_k, lnx_g, lnx_b, q_norm_g, k_norm_g, lam_q1, lam_k1, lam_q2, lam_k2, subln_g, w_branch_rwkv, w_branch_diff, w_out, norm2_g, w_up, w_down, ple_norm_g, w_ple_gate, w_ple_proj):
    weights = dict(norm1_g=norm1_g, w_in=w_in, tm_mu=tm_mu, w_decay_up=w_decay_up, w0=w0, a_up=a_up, a0=a0,
                   g_up=g_up, k_k=k_k, k_a=k_a, r_k=r_k, lnx_g=lnx_g, lnx_b=lnx_b, q_norm_g=q_norm_g,
                   k_norm_g=k_norm_g, lam_q1=lam_q1, lam_k1=lam_k1, lam_q2=lam_q2, lam_k2=lam_k2,
                   subln_g=subln_g, w_branch_rwkv=w_branch_rwkv, w_branch_diff=w_branch_diff, w_out=w_out,
                   norm2_g=norm2_g, w_up=w_up, w_down=w_down, ple_norm_g=ple_norm_g, w_ple_gate=w_ple_gate,
                   w_ple_proj=w_ple_proj)
    depth = w_in.shape[0]
    hp, hs = x_prompt, x_sample
    outs = [[] for _ in range(8)]
    for i in range(depth):
        res = _layer(i, hp, hs, p_prompt[i], p_sample[i], cache_k[i], cache_v[i], state_rwkv[i], state_shift[i],
                     {name: val[i] for name, val in weights.items()})
        hp, hs = res[0], res[1]
        for acc, val in zip(outs, res[2:]):
            acc.append(val)
    return (hp, hs) + tuple(jnp.stack(acc, 0) for acc in outs)
```

```python
import functools
import math

import jax
import jax.numpy as jnp
from jax import lax
from jax.experimental import pallas as pl
from jax.experimental.pallas import tpu as pltpu

F32 = jnp.float32
BF16 = jnp.bfloat16

LANES = 128
HEAD = 64
RMS_EPS = 1e-6
GN_EPS = HEAD * 1e-5
NEG_INF = -1e30
CHUNK = 64
ROT_DIM = HEAD // 4
ROPE_THETA = 500000.0
SCAN_L = 128
SCAN_PAIRS = 8
ATTN_HEADS = 4
SEG_ALIGN = 512
VMEM_LIMIT = 56 * 1024 * 1024
NT = (((1,), (1,)), ((), ()))


def _round_up(n, m):
    return (n + m - 1) // m * m


def _pick(n, target, mult):
    best = None
    for d in range(mult, min(n, target) + 1, mult):
        if n % d == 0:
            best = d
    assert best is not None, (n, target, mult)
    return best


def _cp(sem):
    return pltpu.CompilerParams(dimension_semantics=sem, vmem_limit_bytes=VMEM_LIMIT)


def _bdot(a, b):
    return jnp.dot(a.astype(BF16), b.astype(BF16), preferred_element_type=F32)


def _bdot_nt(a, b):
    return lax.dot_general(a.astype(BF16), b.astype(BF16), NT, preferred_element_type=F32)


def _head_ones():
    r = lax.broadcasted_iota(jnp.int32, (2 * LANES, LANES), 0)
    c = lax.broadcasted_iota(jnp.int32, (2 * LANES, LANES), 1)
    r = jnp.where(r >= LANES, r - LANES, r)
    return jnp.where((r >= HEAD) == (c >= HEAD), 1.0, 0.0).astype(BF16)


def _head_allsum(x, ones2):
    hi = x.astype(BF16)
    lo = (x - hi.astype(F32)).astype(BF16)
    return jnp.dot(jnp.concatenate([hi, lo], axis=1), ones2, preferred_element_type=F32)


def _rmsnorm_kernel(x_ref, g_ref, o_ref):
    x = x_ref[...]
    ms = jnp.mean(x * x, axis=-1, keepdims=True)
    o_ref[...] = (x * lax.rsqrt(ms + RMS_EPS) * g_ref[...]).astype(o_ref.dtype)


def _rmsnorm(x, g):
    M, D = x.shape
    tr = _pick(M, 256, 16)
    return pl.pallas_call(
        _rmsnorm_kernel, grid=(M // tr,),
        in_specs=[pl.BlockSpec((tr, D), lambda i: (i, 0)), pl.BlockSpec((1, D), lambda i: (0, 0))],
        out_specs=pl.BlockSpec((tr, D), lambda i: (i, 0)),
        out_shape=jax.ShapeDtypeStruct((M, D), BF16),
        compiler_params=_cp(("parallel",)), name="rmsnorm")(x, g.reshape(1, D))


def _mm_relu2_kernel(a_ref, b_ref, o_ref):
    p = jnp.maximum(_bdot(a_ref[...], b_ref[...]), 0.0)
    o_ref[...] = (p * p).astype(o_ref.dtype)


def _mm_res_kernel(a_ref, b_ref, r_ref, o_ref):
    o_ref[...] = r_ref[...] + _bdot(a_ref[...], b_ref[...])


def _mm_fullk(kernel, a, b, out_dtype, res=None, tn_target=256, name="mm"):
    M, K = a.shape
    N = b.shape[1]
    tm = _pick(M, 1408, 16)
    tn = _pick(N, tn_target, LANES)
    in_specs = [pl.BlockSpec((tm, K), lambda i, j: (i, 0)), pl.BlockSpec((K, tn), lambda i, j: (0, j))]
    args = [a, b]
    if res is not None:
        in_specs.append(pl.BlockSpec((tm, tn), lambda i, j: (i, j)))
        args.append(res)
    return pl.pallas_call(
        kernel, grid=(M // tm, N // tn), in_specs=in_specs,
        out_specs=pl.BlockSpec((tm, tn), lambda i, j: (i, j)),
        out_shape=jax.ShapeDtypeStruct((M, N), out_dtype),
        compiler_params=_cp(("parallel", "parallel")), name=name)(*args)


def _mm_nt_kernel(a_ref, bt_ref, o_ref):
    o_ref[...] = _bdot_nt(a_ref[...], bt_ref[...])


def _mm_nt_shift_kernel(a_ref, b0_ref, b1_ref, o_ref, *, shift):
    tn = o_ref.shape[1]
    w = jnp.concatenate([b0_ref[...], b1_ref[...]], axis=0)
    o_ref[...] = _bdot_nt(a_ref[...], w[shift:shift + tn])


def _mm_nt_rows(a, bt, row0, n_rows, name):
    M, K = a.shape
    tm = _pick(M, 1408, 16)
    base = row0 // LANES * LANES
    shift = row0 - base
    assert shift % 8 == 0 and n_rows % LANES == 0
    tn = LANES * math.gcd(math.gcd(base // LANES, n_rows // LANES), 2)
    b0 = base // tn
    a_spec = pl.BlockSpec((tm, K), lambda i, j: (i, 0))
    if shift == 0:
        kernel, b_specs, b_args = _mm_nt_kernel, [pl.BlockSpec((tn, K), lambda i, j: (b0 + j, 0))], [bt]
    else:
        b1 = (base + tn) // LANES
        step = tn // LANES
        kernel = functools.partial(_mm_nt_shift_kernel, shift=shift)
        b_specs = [pl.BlockSpec((tn, K), lambda i, j: (b0 + j, 0)),
                   pl.BlockSpec((LANES, K), lambda i, j: (b1 + j * step, 0))]
        b_args = [bt, bt]
    return pl.pallas_call(
        kernel, grid=(M // tm, n_rows // tn), in_specs=[a_spec] + b_specs,
        out_specs=pl.BlockSpec((tm, tn), lambda i, j: (i, j)),
        out_shape=jax.ShapeDtypeStruct((M, n_rows), F32),
        compiler_params=_cp(("parallel", "parallel")), name=name)(a, *b_args)


def _mm_res_kloop_kernel(a_ref, b_ref, r_ref, o_ref):
    @pl.when(pl.program_id(2) == 0)
    def _():
        o_ref[...] = r_ref[...]

    o_ref[...] += _bdot(a_ref[...], b_ref[...])


def _mm_res_kloop(a, b, res, name):
    M, K = a.shape
    N = b.shape[1]
    tm = _pick(M, 1408, 16)
    tn = _pick(N, 512, LANES)
    tk = _pick(K, 2048, LANES)
    return pl.pallas_call(
        _mm_res_kloop_kernel, grid=(M // tm, N // tn, K // tk),
        in_specs=[pl.BlockSpec((tm, tk), lambda i, j, k: (i, k)),
                  pl.BlockSpec((tk, tn), lambda i, j, k: (k, j)),
                  pl.BlockSpec((tm, tn), lambda i, j, k: (i, j))],
        out_specs=pl.BlockSpec((tm, tn), lambda i, j, k: (i, j)),
        out_shape=jax.ShapeDtypeStruct((M, N), F32),
        compiler_params=_cp(("parallel", "parallel", "arbitrary")), name=name)(a, b, res)


def _merge_kernel(yr_ref, yd_ref, wr_ref, wd_ref, gr_ref, gd_ref, o_ref):
    pr = _bdot(yr_ref[...], wr_ref[...])
    pd = _bdot(yd_ref[...], wd_ref[...])
    o_ref[...] = (jax.nn.sigmoid(gr_ref[...]) * pr + jax.nn.sigmoid(gd_ref[...]) * pd).astype(o_ref.dtype)


def _merge(y_r, y_d, w_r, w_d, proj, gate_off):
    M, Kr = y_r.shape
    Kd = y_d.shape[1]
    D = w_r.shape[1]
    tm = _pick(M, 1408, 16)
    tn = _pick(D, 256, LANES)
    assert gate_off % tn == 0
    g0 = gate_off // tn
    g1 = (gate_off + D) // tn
    return pl.pallas_call(
        _merge_kernel, grid=(M // tm, D // tn),
        in_specs=[pl.BlockSpec((tm, Kr), lambda i, j: (i, 0)),
                  pl.BlockSpec((tm, Kd), lambda i, j: (i, 0)),
                  pl.BlockSpec((Kr, tn), lambda i, j: (0, j)),
                  pl.BlockSpec((Kd, tn), lambda i, j: (0, j)),
                  pl.BlockSpec((tm, tn), lambda i, j: (i, g0 + j)),
                  pl.BlockSpec((tm, tn), lambda i, j: (i, g1 + j))],
        out_specs=pl.BlockSpec((tm, tn), lambda i, j: (i, j)),
        out_shape=jax.ShapeDtypeStruct((M, D), BF16),
        compiler_params=_cp(("parallel", "parallel")), name="branch_merge")(y_r, y_d, w_r, w_d, proj, proj)


def _ple_kernel(h_ref, wg_ref, pe_ref, wp_ref, x_ref, o_ref):
    gate = jax.nn.sigmoid(_bdot(h_ref[...], wg_ref[...]))
    o_ref[...] = x_ref[...] + gate * _bdot(pe_ref[...], wp_ref[...])


def _ple(h, x, pe, w_gate, w_proj, row_off, rows):
    D = x.shape[1]
    P = pe.shape[1]
    tm = _pick(rows, 1024, 16)
    tn = _pick(D, 256, LANES)
    assert row_off % tm == 0
    r0 = row_off // tm
    return pl.pallas_call(
        _ple_kernel, grid=(rows // tm, D // tn),
        in_specs=[pl.BlockSpec((tm, D), lambda i, j: (r0 + i, 0)),
                  pl.BlockSpec((D, tn), lambda i, j: (0, j)),
                  pl.BlockSpec((tm, P), lambda i, j: (i, 0)),
                  pl.BlockSpec((P, tn), lambda i, j: (0, j)),
                  pl.BlockSpec((tm, tn), lambda i, j: (r0 + i, j))],
        out_specs=pl.BlockSpec((tm, tn), lambda i, j: (i, j)),
        out_shape=jax.ShapeDtypeStruct((rows, D), F32),
        compiler_params=_cp(("parallel", "parallel")), name="ple")(h, w_gate, pe, w_proj, x)


def _head_allsum_wide(x, ones2):
    n = x.shape[1] // LANES
    rows = x.shape[0]
    s = _head_allsum(jnp.concatenate([x[:, i * LANES:(i + 1) * LANES] for i in range(n)], axis=0), ones2)
    return jnp.concatenate([s[i * rows:(i + 1) * rows] for i in range(n)], axis=1)


def _rwkv_kernel(pr_ref, pk_ref, pv_ref, pw_ref, spr_ref, spk_ref, spv_ref, spw_ref,
                 mur_ref, muk_ref, muv_ref, muw_ref, w0_ref, a0_ref, kk_ref, ka_ref, rk_ref,
                 wd_ref, wa_ref, wg_ref, lng_ref, lnb_ref, s0_ref,
                 y_ref, so_ref, sbd_ref, prev_ref, *, PB, L, win_w, win_a, win_g):
    c = pl.program_id(2)
    Tb = pr_ref.shape[0]
    W = PB * LANES
    lane = lax.broadcasted_iota(jnp.int32, (L, LANES), 1)
    lane2 = lax.broadcasted_iota(jnp.int32, (2 * L, LANES), 1)
    row = lax.broadcasted_iota(jnp.int32, (L, L), 0)
    col = lax.broadcasted_iota(jnp.int32, (L, L), 1)
    incl = row >= col
    strict = row > col
    tri = jnp.where(incl, 1.0, 0.0).astype(BF16)
    eye = jnp.where(row == col, 1.0, 0.0)
    srow = lax.broadcasted_iota(jnp.int32, (LANES, LANES), 0)
    scol = lax.broadcasted_iota(jnp.int32, (LANES, LANES), 1)
    same_head = (srow >= HEAD) == (scol >= HEAD)
    ones2 = _head_ones()

    @pl.when(c == 0)
    def _():
        sbd_ref[...] = s0_ref[0]
        prev_ref[:, 0:W] = spr_ref[0]
        prev_ref[:, W:2 * W] = spk_ref[0]
        prev_ref[:, 2 * W:3 * W] = spv_ref[0]
        prev_ref[:, 3 * W:] = spw_ref[0]

    first_row = lax.broadcasted_iota(jnp.int32, (Tb, 1), 0) == 0

    def shifted(x_ref, mu_ref, off):
        x = x_ref[...]
        prev = prev_ref[:, off:off + x.shape[1]]
        prev_ref[:, off:off + x.shape[1]] = x[Tb - 1:Tb, :]
        sh = jnp.where(first_row, prev, pltpu.roll(x, 1, axis=0))
        return x + (sh - x) * mu_ref[...]

    r = shifted(pr_ref, mur_ref, 0)
    k = shifted(pk_ref, muk_ref, W)
    v = shifted(pv_ref, muv_ref, 2 * W)
    lo = shifted(pw_ref, muw_ref, 3 * W)
    z = w0_ref[...] + _bdot(jnp.tanh(lo[:, win_w[0]:win_w[1]]), wd_ref[...])
    lw = -math.exp(-0.5) * jax.nn.sigmoid(z)
    alr = jax.nn.sigmoid(a0_ref[...] + _bdot(lo[:, win_a[0]:win_a[1]], wa_ref[...]))
    gate = _bdot(jax.nn.sigmoid(lo[:, win_g[0]:win_g[1]]), wg_ref[...])
    kk = k * kk_ref[...]
    k2 = k * (1.0 + (alr - 1.0) * ka_ref[...])
    kkn = kk * lax.rsqrt(_head_allsum_wide(kk * kk, ones2) + 1e-12)
    bonus_gate = _head_allsum_wide(r * k2 * rk_ref[...], ones2) * v * gate
    if Tb < L:
        pad = lambda x: jnp.concatenate([x, jnp.zeros((L - Tb, W), F32)], axis=0)
        r, k2, v, lw, kkn, alr = pad(r), pad(k2), pad(v), pad(lw), pad(kkn), pad(alr)

    pairs = range(PB)
    heads = [(p, e) for p in pairs for e in range(2)]
    cols = [slice(p * LANES, (p + 1) * LANES) for p in pairs]
    bmm = lambda x, y: lax.dot_general(x.astype(BF16), y.astype(BF16), (((2,), (1,)), ((0,), (0,))),
                                       preferred_element_type=F32)
    bmm_nt = lambda x, y: lax.dot_general(x.astype(BF16), y.astype(BF16), (((2,), (2,)), ((0,), (0,))),
                                          preferred_element_type=F32)
    rows_cat = lambda xs: jnp.concatenate(xs, axis=0)

    w1 = lw.astype(BF16)
    res1 = lw - w1.astype(F32)
    w2 = res1.astype(BF16)
    w3 = (res1 - w2.astype(F32)).astype(BF16)
    lc3 = jnp.dot(tri, jnp.concatenate([w1, w2, w3], axis=1), preferred_element_type=F32)
    lc = lc3[:, :W] + lc3[:, W:2 * W] + lc3[:, 2 * W:]
    lc_last = lc[L - 1:L, :]
    g_in, g_prev, g_inv = jnp.exp(lc), jnp.exp(lc - lw), jnp.exp(-lc)
    ratio = jnp.exp(lc_last - lc)
    g_last = jnp.exp(lc_last)
    b = kkn * alr
    At, Rt, Bt, Kt = -kkn * g_prev, r * g_in, b * g_inv, k2 * g_inv
    Bh, Kh = b * ratio, k2 * ratio

    he = lane >= HEAD
    own = [lane2 < HEAD, lane2 >= HEAD]
    X = [rows_cat([At[:, cs], Rt[:, cs]]) for cs in cols]
    Y = [rows_cat([Bt[:, cs], Kt[:, cs]]) for cs in cols]
    G = bmm_nt(jnp.stack([jnp.where(own[e], X[p], 0.0) for p, e in heads]),
               jnp.stack([Y[p] for p, e in heads]))
    Nab = jnp.where(strict, G[:, :L, :L], 0.0)
    Aak = jnp.where(strict, G[:, :L, L:], 0.0)
    Arb = jnp.where(incl, G[:, L:, :L], 0.0)
    Ark = jnp.where(incl, G[:, L:, L:], 0.0)
    Tm = eye + Nab
    Np = Nab
    lev = 1
    while lev * 2 < L:
        Np = bmm(Np, Np)
        Tm = Tm + bmm(Tm, Np)
        lev *= 2
    AV = bmm(jnp.concatenate([Aak, Ark], axis=1), jnp.stack([v[:, cols[p]] for p, e in heads]))
    TW = bmm(Tm, jnp.concatenate([jnp.stack([At[:, cols[p]] for p, e in heads]), AV[:, :L]], axis=2))
    pick = lambda x, p: jnp.where(he, x[2 * p + 1], x[2 * p])
    W1 = [pick(TW[:, :, :LANES], p) for p in pairs]
    W2 = [pick(TW[:, :, LANES:], p) for p in pairs]
    Op = [pick(AV[:, L:], p) for p in pairs]
    S = sbd_ref[...]
    XS = bmm_nt(jnp.stack([rows_cat([W1[p], Rt[:, cols[p]]]) for p in pairs]), S)
    U = [XS[p, :L] + W2[p] for p in pairs]
    AU = bmm(Arb, jnp.stack([U[p] for p, e in heads]))
    O = [XS[p, L:] + Op[p] + pick(AU, p) for p in pairs]
    upd = bmm(jnp.stack([rows_cat([U[p], v[:, cols[p]]]).T for p in pairs]),
              jnp.stack([rows_cat([Bh[:, cols[p]], Kh[:, cols[p]]]) for p in pairs]))
    sbd_ref[...] = S * jnp.stack([g_last[:, cols[p]] for p in pairs]) + jnp.where(same_head, upd, 0.0)
    Oc = rows_cat(O)
    mu = _head_allsum(Oc, ones2) * (1.0 / HEAD)
    d = Oc - mu
    var = _head_allsum(d * d, ones2) * (1.0 / HEAD)
    nrm = d * lax.rsqrt(var + GN_EPS)
    for p in pairs:
        gn = nrm[p * L:p * L + Tb] * lng_ref[:, cols[p]] + lnb_ref[:, cols[p]]
        y_ref[:, cols[p]] = (gn * gate[:, cols[p]] + bonus_gate[:, cols[p]]).astype(y_ref.dtype)

    @pl.when(c == pl.num_programs(2) - 1)
    def _():
        so_ref[0] = sbd_ref[...]


def _lora_window(start, width, w, base):
    s0 = start // LANES * LANES
    s1 = _round_up(start + width, LANES)
    wp = jnp.zeros((s1 - s0, w.shape[1]), w.dtype).at[start - s0:start - s0 + width].set(w)
    return (s0 - base, s1 - base), wp


def _rwkv(proj, row_off, B, T, SWp, shift_prev, s0_bd, mu, w0, a0, k_k, k_a, r_k, w_decay_up, a_up, g_up,
          lnx_g, lnx_b):
    RW = w0.shape[-1]
    L = SCAN_L
    Tb = min(T, L)
    assert T % Tb == 0 and row_off % Tb == 0
    nc = T // Tb
    r0 = row_off // Tb
    npair = RW // LANES
    PB = math.gcd(npair, SCAN_PAIRS)
    W = PB * LANES
    ng = RW // W
    LW = SWp - 3 * RW
    assert LW % LANES == 0 and (3 * RW) % LW == 0
    lb = 3 * RW // LW
    DL, AL, GL = w_decay_up.shape[0], a_up.shape[0], g_up.shape[0]
    win_w, wd = _lora_window(3 * RW, DL, w_decay_up, 3 * RW)
    win_a, wa = _lora_window(3 * RW + DL, AL, a_up, 3 * RW)
    win_g, wg = _lora_window(3 * RW + DL + AL, GL, g_up, 3 * RW)
    row = lambda x: x.reshape(1, -1)
    seg = lambda s: pl.BlockSpec((Tb, W), lambda b, g, c: (r0 + b * nc + c, s * ng + g))
    seg_prev = lambda s: pl.BlockSpec((1, 1, W), lambda b, g, c: (b, 0, s * ng + g))
    seg_mu = lambda s: pl.BlockSpec((1, W), lambda b, g, c: (0, s * ng + g))
    vec = pl.BlockSpec((1, W), lambda b, g, c: (0, g))
    wblk = lambda w: pl.BlockSpec((w.shape[0], W), lambda b, g, c: (0, g))
    sblk = pl.BlockSpec((1, PB, LANES, LANES), lambda b, g, c: (b, g, 0, 0))
    mu2 = row(mu)
    y, s_out = pl.pallas_call(
        functools.partial(_rwkv_kernel, PB=PB, L=L, win_w=win_w, win_a=win_a, win_g=win_g),
        grid=(B, ng, nc),
        in_specs=[seg(0), seg(1), seg(2), pl.BlockSpec((Tb, LW), lambda b, g, c: (r0 + b * nc + c, lb)),
                  seg_prev(0), seg_prev(1), seg_prev(2), pl.BlockSpec((1, 1, LW), lambda b, g, c: (b, 0, lb)),
                  seg_mu(0), seg_mu(1), seg_mu(2), pl.BlockSpec((1, LW), lambda b, g, c: (0, lb)),
                  vec, vec, vec, vec, vec, wblk(wd), wblk(wa), wblk(wg), vec, vec, sblk],
        out_specs=[pl.BlockSpec((Tb, W), lambda b, g, c: (b * nc + c, g)), sblk],
        out_shape=[jax.ShapeDtypeStruct((B * T, RW), BF16),
                   jax.ShapeDtypeStruct((B, npair, LANES, LANES), F32)],
        scratch_shapes=[pltpu.VMEM((PB, LANES, LANES), F32), pltpu.VMEM((1, 3 * W + LW), F32)],
        compiler_params=_cp(("parallel", "parallel", "arbitrary")), name="rwkv",
    )(proj, proj, proj, proj, shift_prev, shift_prev, shift_prev, shift_prev, mu2, mu2, mu2, mu2,
      row(w0), row(a0), row(k_k), row(k_a), row(r_k), wd, wa, wg, row(lnx_g), row(lnx_b), s0_bd)
    return y, s_out


def _to_blockdiag(s):
    B, H = s.shape[:2]
    s = s.reshape(B, H // 2, 2, HEAD, HEAD)
    z = jnp.zeros_like(s[:, :, 0])
    top = jnp.concatenate([s[:, :, 0], z], axis=-1)
    bot = jnp.concatenate([z, s[:, :, 1]], axis=-1)
    return jnp.concatenate([top, bot], axis=-2)


def _from_blockdiag(sbd):
    B, P = sbd.shape[:2]
    return jnp.stack([sbd[:, :, :HEAD, :HEAD], sbd[:, :, HEAD:, HEAD:]], axis=2).reshape(B, 2 * P, HEAD, HEAD)


def _qknorm_kernel(q_ref, k_ref, v_ref, c_ref, s1_ref, s2_ref, qg_ref, kg_ref,
                   qb_o, kf_o, kb_o, vf_o, vb_o):
    cos, s1, s2 = c_ref[...], s1_ref[...], s2_ref[...]
    ones2 = _head_ones()

    def norm_rope(x, gain):
        ms = _head_allsum(x * x, ones2) * (1.0 / HEAD)
        y = x * lax.rsqrt(ms + RMS_EPS) * gain
        half = ROT_DIM // 2
        return y * cos + pltpu.roll(y, LANES - half, axis=1) * s1 + pltpu.roll(y, half, axis=1) * s2

    for c in range(0, q_ref.shape[1], LANES):
        cs = slice(c, c + LANES)
        q = norm_rope(q_ref[:, cs], qg_ref[...])
        k = norm_rope(k_ref[:, cs], kg_ref[...])
        qb_o[:, cs] = (q * (HEAD ** -0.5)).astype(qb_o.dtype)
        kf_o[:, cs] = k
        kb_o[:, cs] = k.astype(kb_o.dtype)
    v = v_ref[...]
    vf_o[...] = v
    vb_o[...] = v.astype(vb_o.dtype)


def _rope_tables(pos):
    half = ROT_DIM // 2
    inv_freq = jnp.power(jnp.float32(ROPE_THETA), -jnp.arange(0, ROT_DIM, 2, dtype=jnp.float32) / ROT_DIM)
    ang = pos.astype(jnp.float32)[:, None] * inv_freq[None, :]
    cos, sin = jnp.cos(ang), jnp.sin(ang)
    T = pos.shape[0]
    pad = jnp.zeros((T, HEAD - ROT_DIM), F32)
    z = jnp.zeros((T, half), F32)
    c = jnp.concatenate([cos, cos, pad + 1.0], axis=1)
    s1 = jnp.concatenate([-sin, z, pad], axis=1)
    s2 = jnp.concatenate([z, sin, pad], axis=1)
    tile = lambda t: jnp.concatenate([t, t], axis=1)
    return tile(c), tile(s1), tile(s2)


def _qknorm(proj, row_off, rows, DW, tables, q_norm_g, k_norm_g):
    tr = _pick(rows, 256, 16)
    cw = _pick(DW, 512, LANES)
    assert row_off % tr == 0
    r0 = row_off // tr
    nc = DW // cw
    gain = lambda g: jnp.concatenate([g, g]).reshape(1, LANES)
    seg = lambda s: pl.BlockSpec((tr, cw), lambda i, j: (r0 + i, s * nc + j))
    tab = pl.BlockSpec((tr, LANES), lambda i, j: (i, 0))
    vec = pl.BlockSpec((1, LANES), lambda i, j: (0, 0))
    out = pl.BlockSpec((tr, cw), lambda i, j: (i, j))
    sds = lambda dt: jax.ShapeDtypeStruct((rows, DW), dt)
    return pl.pallas_call(
        _qknorm_kernel, grid=(rows // tr, nc),
        in_specs=[seg(0), seg(1), seg(2), tab, tab, tab, vec, vec],
        out_specs=[out] * 5,
        out_shape=[sds(BF16), sds(F32), sds(BF16), sds(F32), sds(BF16)],
        compiler_params=_cp(("parallel", "parallel")), name="qk_norm_rope",
    )(proj, proj, proj, *tables, gain(q_norm_g), gain(k_norm_g))


def _lambda(lam_ref, lam_init):
    lam = lam_ref[...]
    return (jnp.exp(jnp.sum(lam[0:1] * lam[1:2], axis=1, keepdims=True))
            - jnp.exp(jnp.sum(lam[2:3] * lam[3:4], axis=1, keepdims=True)) + lam_init)


def _stack_components(q):
    lane = lax.broadcasted_iota(jnp.int32, q.shape, 1)
    zero = jnp.zeros_like(q)
    return jnp.concatenate([jnp.where(lane < HEAD, q, zero), jnp.where(lane >= HEAD, q, zero)], axis=0)


def _softmax_step(qq, k, v, m_ref, l_ref, acc_ref, mask):
    s = lax.dot_general(qq, k, (((2,), (2,)), ((0,), (0,))), preferred_element_type=F32)
    if mask is not None:
        s = jnp.where(mask, s, NEG_INF)
    m_old = m_ref[...]
    m_new = jnp.maximum(m_old, jnp.max(s, axis=2, keepdims=True))
    alpha = jnp.exp(m_old - m_new)
    tk = s.shape[2]
    p = jnp.exp(s - jnp.concatenate([m_new] * pl.cdiv(tk, LANES), axis=2)[:, :, :tk])
    v1 = jnp.concatenate([v, jnp.ones(v.shape, v.dtype)], axis=2)
    pv = lax.dot_general(p.astype(BF16), v1, (((2,), (1,)), ((0,), (0,))), preferred_element_type=F32)
    l_ref[...] = alpha * l_ref[...] + pv[:, :, LANES:]
    acc_ref[...] = alpha * acc_ref[...] + pv[:, :, :LANES]
    m_ref[...] = m_new


def _finish_head(t, m_ref, l_ref, acc_ref, lam, g, lam_init):
    o = acc_ref[...] / l_ref[...]
    o = o[:t] - lam * o[t:]
    ms = jnp.mean(o * o, axis=-1, keepdims=True)
    return o * lax.rsqrt(ms + RMS_EPS) * g * (1.0 - lam_init)


def _attn_prompt_kernel(lam_ref, g_ref, q_ref, k_ref, v_ref, o_ref, m_ref, l_ref, acc_ref, *, lam_init):
    qi = pl.program_id(2)
    tq = q_ref.shape[0]
    nh = q_ref.shape[1] // LANES
    cols = [slice(h * LANES, (h + 1) * LANES) for h in range(nh)]
    qq = jnp.stack([_stack_components(q_ref[:, cs]) for cs in cols])
    m_ref[...] = jnp.full(m_ref.shape, NEG_INF, F32)
    l_ref[...] = jnp.zeros(l_ref.shape, F32)
    acc_ref[...] = jnp.zeros(acc_ref.shape, F32)

    def block(j, mask):
        rows = pl.ds(pl.multiple_of(j * tq, tq), tq)
        _softmax_step(qq, jnp.stack([k_ref[rows, cs] for cs in cols]), jnp.stack([v_ref[rows, cs] for cs in cols]),
                      m_ref, l_ref, acc_ref, mask)

    def body(j, carry):
        block(j, None)
        return carry

    lax.fori_loop(0, qi, body, 0)
    r = lax.broadcasted_iota(jnp.int32, (2 * tq, tq), 0)
    cidx = lax.broadcasted_iota(jnp.int32, (2 * tq, tq), 1)
    r = jnp.where(r >= tq, r - tq, r)
    block(qi, (r // CHUNK) >= (cidx // CHUNK))
    lam = _lambda(lam_ref, lam_init)
    for h, cs in enumerate(cols):
        o_ref[:, cs] = _finish_head(tq, m_ref.at[h], l_ref.at[h], acc_ref.at[h], lam, g_ref[...],
                                    lam_init).astype(o_ref.dtype)


def _attn_prompt(qb, kb, vb, lam_vecs, subln_g, B, T, lam_init):
    DW = qb.shape[1]
    nh = DW // LANES
    hg = math.gcd(nh, ATTN_HEADS)
    tq = _pick(T, 256, CHUNK)
    nq = T // tq
    return pl.pallas_call(
        functools.partial(_attn_prompt_kernel, lam_init=lam_init), grid=(B, nh // hg, nq),
        in_specs=[pl.BlockSpec(lam_vecs.shape, lambda b, h, i: (0, 0)),
                  pl.BlockSpec((1, LANES), lambda b, h, i: (0, 0)),
                  pl.BlockSpec((tq, hg * LANES), lambda b, h, i: (b * nq + i, h)),
                  pl.BlockSpec((T, hg * LANES), lambda b, h, i: (b, h)),
                  pl.BlockSpec((T, hg * LANES), lambda b, h, i: (b, h))],
        out_specs=pl.BlockSpec((tq, hg * LANES), lambda b, h, i: (b * nq + i, h)),
        out_shape=jax.ShapeDtypeStruct((B * T, DW), BF16),
        scratch_shapes=[pltpu.VMEM((hg, 2 * tq, LANES), F32)] * 3,
        compiler_params=_cp(("parallel", "parallel", "arbitrary")), name="attn_prompt",
    )(lam_vecs, subln_g.reshape(1, LANES), qb, kb, vb)


def _attn_sample_kernel(lam_ref, g_ref, q_ref, kc_ref, vc_ref, kn_ref, vn_ref, o_ref, m_ref, l_ref, acc_ref,
                        *, lam_init, past):
    j = pl.program_id(1)
    ts = q_ref.shape[0]
    nh = q_ref.shape[1] // LANES

    @pl.when(j == 0)
    def _():
        m_ref[...] = jnp.full(m_ref.shape, NEG_INF, F32)
        l_ref[...] = jnp.zeros(l_ref.shape, F32)
        acc_ref[...] = jnp.zeros(acc_ref.shape, F32)

    cols = [slice(h * LANES, (h + 1) * LANES) for h in range(nh)]
    tk = kc_ref.shape[0] // nh

    def heads(head_k, head_v, mask):
        qq = jnp.stack([_stack_components(q_ref[:, cs]) for cs in cols])
        _softmax_step(qq, jnp.stack([head_k(h).astype(BF16) for h in range(nh)]),
                      jnp.stack([head_v(h).astype(BF16) for h in range(nh)]), m_ref, l_ref, acc_ref, mask)

    heads(lambda h: kc_ref[pl.ds(h, tk, stride=nh), :], lambda h: vc_ref[pl.ds(h, tk, stride=nh), :], None)

    @pl.when(j == pl.num_programs(1) - 1)
    def _():
        r = lax.broadcasted_iota(jnp.int32, (2 * ts, ts), 0)
        cidx = lax.broadcasted_iota(jnp.int32, (2 * ts, ts), 1)
        r = jnp.where(r >= ts, r - ts, r)
        heads(lambda h: kn_ref[:, cols[h]], lambda h: vn_ref[:, cols[h]],
              ((past + r) // CHUNK) >= ((past + cidx) // CHUNK))
        lam = _lambda(lam_ref, lam_init)
        for h in range(nh):
            cs = slice(h * LANES, (h + 1) * LANES)
            o_ref[:, cs] = _finish_head(ts, m_ref.at[h], l_ref.at[h], acc_ref.at[h], lam, g_ref[...],
                                        lam_init).astype(o_ref.dtype)


def _attn_sample(qb, kb, vb, cache_k, cache_v, lam_vecs, subln_g, B, Ts, lam_init):
    DW = qb.shape[1]
    nh = DW // LANES
    past = cache_k.shape[0] // (B * nh)
    tk = _pick(past, 512, 16)
    nkv = past // tk
    new = pl.BlockSpec((Ts, DW), lambda b, j: (b, 0))
    old = pl.BlockSpec((tk * nh, LANES), lambda b, j: (b * nkv + j, 0))
    return pl.pallas_call(
        functools.partial(_attn_sample_kernel, lam_init=lam_init, past=past), grid=(B, nkv),
        in_specs=[pl.BlockSpec(lam_vecs.shape, lambda b, j: (0, 0)),
                  pl.BlockSpec((1, LANES), lambda b, j: (0, 0)),
                  new, old, old, new, new],
        out_specs=new,
        out_shape=jax.ShapeDtypeStruct((B * Ts, DW), BF16),
        scratch_shapes=[pltpu.VMEM((nh, 2 * Ts, LANES), F32)] * 3,
        compiler_params=_cp(("parallel", "arbitrary")), name="attn_sample",
    )(lam_vecs, subln_g.reshape(1, LANES), qb, cache_k, cache_v, kb, vb)


def _layer(i, x_prompt, x_sample, pe_prompt, pe_sample, cache_k, cache_v, state_rwkv, state_shift, w):
    Bp, Tp, D = x_prompt.shape
    Bs, Ts, _ = x_sample.shape
    Mp, Ms = Bp * Tp, Bs * Ts
    past = cache_k.shape[1]
    RW = w["w0"].shape[-1]
    DW = w["w_branch_diff"].shape[0]
    SW = state_shift.shape[-1]
    SWp = _round_up(SW, SEG_ALIGN)
    w_in = w["w_in"]
    lam_init = 0.8 - 0.6 * math.exp(-0.3 * i)

    x_all = jnp.concatenate([x_prompt.reshape(Mp, D), x_sample.reshape(Ms, D)], axis=0)
    h1 = _rmsnorm(x_all, w["norm1_g"])
    w_t = w_in.T
    proj_r = _mm_nt_rows(h1, w_t, 0, SWp, name="proj_rwkv")
    proj_a = _mm_nt_rows(h1, w_t, SW, w_t.shape[0] - SW, name="proj_attn")

    pad_sw = lambda a: jnp.pad(a, ((0, 0), (0, 0), (0, SWp - SW)))
    mu = jnp.pad(w["tm_mu"], (0, SWp - SW))
    lam_vecs = jnp.stack([w["lam_q1"], w["lam_k1"], w["lam_q2"], w["lam_k2"]])
    rwkv_w = (mu, w["w0"], w["a0"], w["k_k"], w["k_a"], w["r_k"].reshape(-1), w["w_decay_up"], w["a_up"], w["g_up"],
              w["lnx_g"], w["lnx_b"])

    s0_p = jnp.zeros((Bp, RW // LANES, LANES, LANES), F32)
    yr_p, s_p = _rwkv(proj_r, 0, Bp, Tp, SWp, jnp.zeros((Bp, 1, SWp), F32), s0_p, *rwkv_w)
    tab_p = _rope_tables(jnp.arange(Tp, dtype=jnp.int32))
    tab_p = tuple(jnp.tile(t, (Bp, 1)) for t in tab_p)
    qb_p, kf_p, kb_p, vf_p, vb_p = _qknorm(proj_a, 0, Mp, DW, tab_p, w["q_norm_g"], w["k_norm_g"])
    yd_p = _attn_prompt(qb_p, kb_p, vb_p, lam_vecs, w["subln_g"], Bp, Tp, lam_init)

    yr_s, s_s = _rwkv(proj_r, Mp, Bs, Ts, SWp, pad_sw(state_shift), _to_blockdiag(state_rwkv), *rwkv_w)
    tab_s = _rope_tables(past + jnp.arange(Ts, dtype=jnp.int32))
    tab_s = tuple(jnp.tile(t, (Bs, 1)) for t in tab_s)
    qb_s, kf_s, kb_s, vf_s, vb_s = _qknorm(proj_a, Mp, Ms, DW, tab_s, w["q_norm_g"], w["k_norm_g"])
    yd_s = _attn_sample(qb_s, kb_s, vb_s, cache_k.reshape(-1, LANES), cache_v.reshape(-1, LANES),
                        lam_vecs, w["subln_g"], Bs, Ts, lam_init)

    y_r = jnp.concatenate([yr_p, yr_s], axis=0)
    y_d = jnp.concatenate([yd_p, yd_s], axis=0)
    merged = _merge(y_r, y_d, w["w_branch_rwkv"], w["w_branch_diff"], proj_a, 3 * DW)
    x1 = _mm_fullk(_mm_res_kernel, merged, w["w_out"], F32, res=x_all, name="out_proj")
    h2 = _rmsnorm(x1, w["norm2_g"])
    u = _mm_fullk(_mm_relu2_kernel, h2, w["w_up"], BF16, tn_target=512, name="ffn_up")
    x2 = _mm_res_kloop(u, w["w_down"], x1, name="ffn_down")
    h3 = _rmsnorm(x2, w["ple_norm_g"])
    y_p = _ple(h3, x2, pe_prompt.reshape(Mp, -1), w["w_ple_gate"], w["w_ple_proj"], 0, Mp)
    y_s = _ple(h3, x2, pe_sample.reshape(Ms, -1), w["w_ple_gate"], w["w_ple_proj"], Mp, Ms)

    nh_d = DW // LANES
    shift_p = proj_r[Tp - 1:Mp:Tp, :SW].reshape(Bp, 1, SW)
    shift_s = proj_r[Mp + Ts - 1::Ts, :SW].reshape(Bs, 1, SW)
    return (y_p.reshape(Bp, Tp, D), y_s.reshape(Bs, Ts, D),
            kf_p.reshape(Bp, Tp, nh_d, LANES), vf_p.reshape(Bp, Tp, nh_d, LANES), _from_blockdiag(s_p), shift_p,
            kf_s.reshape(Bs, Ts, nh_d, LANES), vf_s.reshape(Bs, Ts, nh_d, LANES), _from_blockdiag(s_s), shift_s)


def kernel(x_prompt, x_sample, p_prompt, p_sample, cache_k, cache_v, state_rwkv, state_shift, norm1_g, w_in, tm_mu, w_decay_up, w0, a_up, a0, g_up, k_k, k_a, r_k, lnx_g, lnx_b, q_norm_g, k_norm_g, lam_q1, lam_k1, lam_q2, lam_k2, subln_g, w_branch_rwkv, w_branch_diff, w_out, norm2_g, w_up, w_down, ple_norm_g, w_ple_gate, w_ple_proj):
    weights = dict(norm1_g=norm1_g, w_in=w_in, tm_mu=tm_mu, w_decay_up=w_decay_up, w0=w0, a_up=a_up, a0=a0,
                   g_up=g_up, k_k=k_k, k_a=k_a, r_k=r_k, lnx_g=lnx_g, lnx_b=lnx_b, q_norm_g=q_norm_g,
                   k_norm_g=k_norm_g, lam_q1=lam_q1, lam_k1=lam_k1, lam_q2=lam_q2, lam_k2=lam_k2,
                   subln_g=subln_g, w_branch_rwkv=w_branch_rwkv, w_branch_diff=w_branch_diff, w_out=w_out,
                   norm2_g=norm2_g, w_up=w_up, w_down=w_down, ple_norm_g=ple_norm_g, w_ple_gate=w_ple_gate,
                   w_ple_proj=w_ple_proj)
    depth = w_in.shape[0]
    hp, hs = x_prompt, x_sample
    outs = [[] for _ in range(8)]
    for i in range(depth):
        res = _layer(i, hp, hs, p_prompt[i], p_sample[i], cache_k[i], cache_v[i], state_rwkv[i], state_shift[i],
                     {name: val[i] for name, val in weights.items()})
        hp, hs = res[0], res[1]
        for acc, val in zip(outs, res[2:]):
            acc.append(val)
    return (hp, hs) + tuple(jnp.stack(acc, 0) for acc in outs)
```

```python
import functools
import math

import jax
import jax.numpy as jnp
from jax import lax
from jax.experimental import pallas as pl
from jax.experimental.pallas import tpu as pltpu

F32 = jnp.float32
BF16 = jnp.bfloat16

LANES = 128
HEAD = 64
RMS_EPS = 1e-6
GN_EPS = HEAD * 1e-5
NEG_INF = -1e30
CHUNK = 64
ROT_DIM = HEAD // 4
ROPE_THETA = 500000.0
SCAN_L = 128
SCAN_PAIRS = 8
ATTN_HEADS = 4
SEG_ALIGN = 512
VMEM_LIMIT = 56 * 1024 * 1024
NT = (((1,), (1,)), ((), ()))


def _round_up(n, m):
    return (n + m - 1) // m * m


def _pick(n, target, mult):
    best = None
    for d in range(mult, min(n, target) + 1, mult):
        if n % d == 0:
            best = d
    assert best is not None, (n, target, mult)
    return best


def _cp(sem):
    return pltpu.CompilerParams(dimension_semantics=sem, vmem_limit_bytes=VMEM_LIMIT)


def _bdot(a, b):
    return jnp.dot(a.astype(BF16), b.astype(BF16), preferred_element_type=F32)


def _bdot_nt(a, b):
    return lax.dot_general(a.astype(BF16), b.astype(BF16), NT, preferred_element_type=F32)


def _head_ones():
    r = lax.broadcasted_iota(jnp.int32, (2 * LANES, LANES), 0)
    c = lax.broadcasted_iota(jnp.int32, (2 * LANES, LANES), 1)
    r = jnp.where(r >= LANES, r - LANES, r)
    return jnp.where((r >= HEAD) == (c >= HEAD), 1.0, 0.0).astype(BF16)


def _head_allsum(x, ones2):
    hi = x.astype(BF16)
    lo = (x - hi.astype(F32)).astype(BF16)
    return jnp.dot(jnp.concatenate([hi, lo], axis=1), ones2, preferred_element_type=F32)


def _rmsnorm_kernel(x_ref, g_ref, o_ref):
    x = x_ref[...]
    ms = jnp.mean(x * x, axis=-1, keepdims=True)
    o_ref[...] = (x * lax.rsqrt(ms + RMS_EPS) * g_ref[...]).astype(o_ref.dtype)


def _rmsnorm(x, g):
    M, D = x.shape
    tr = _pick(M, 256, 16)
    return pl.pallas_call(
        _rmsnorm_kernel, grid=(M // tr,),
        in_specs=[pl.BlockSpec((tr, D), lambda i: (i, 0)), pl.BlockSpec((1, D), lambda i: (0, 0))],
        out_specs=pl.BlockSpec((tr, D), lambda i: (i, 0)),
        out_shape=jax.ShapeDtypeStruct((M, D), BF16),
        compiler_params=_cp(("parallel",)), name="rmsnorm")(x, g.reshape(1, D))


def _mm_relu2_kernel(a_ref, b_ref, o_ref):
    p = jnp.maximum(_bdot(a_ref[...], b_ref[...]), 0.0)
    o_ref[...] = (p * p).astype(o_ref.dtype)


def _mm_res_kernel(a_ref, b_ref, r_ref, o_ref):
    o_ref[...] = r_ref[...] + _bdot(a_ref[...], b_ref[...])


def _mm_fullk(kernel, a, b, out_dtype, res=None, tm_target=1408, tn_target=512, name="mm"):
    M, K = a.shape
    N = b.shape[1]
    tm = _pick(M, tm_target, 16)
    tn = _pick(N, tn_target, LANES)
    in_specs = [pl.BlockSpec((tm, K), lambda i, j: (i, 0)), pl.BlockSpec((K, tn), lambda i, j: (0, j))]
    args = [a, b]
    if res is not None:
        in_specs.append(pl.BlockSpec((tm, tn), lambda i, j: (i, j)))
        args.append(res)
    return pl.pallas_call(
        kernel, grid=(M // tm, N // tn), in_specs=in_specs,
        out_specs=pl.BlockSpec((tm, tn), lambda i, j: (i, j)),
        out_shape=jax.ShapeDtypeStruct((M, N), out_dtype),
        compiler_params=_cp(("parallel", "parallel")), name=name)(*args)


def _mm_nt_kernel(a_ref, bt_ref, o_ref):
    o_ref[...] = _bdot_nt(a_ref[...], bt_ref[...])


def _mm_nt_shift_kernel(a_ref, b0_ref, b1_ref, o_ref, *, shift):
    tn = o_ref.shape[1]
    w = jnp.concatenate([b0_ref[...], b1_ref[...]], axis=0)
    o_ref[...] = _bdot_nt(a_ref[...], w[shift:shift + tn])


def _mm_nt_rows(a, bt, row0, n_rows, name):
    M, K = a.shape
    tm = _pick(M, 1408, 16)
    base = row0 // LANES * LANES
    shift = row0 - base
    assert shift % 8 == 0 and n_rows % LANES == 0
    tn = LANES * math.gcd(math.gcd(base // LANES, n_rows // LANES), 2)
    b0 = base // tn
    a_spec = pl.BlockSpec((tm, K), lambda i, j: (i, 0))
    if shift == 0:
        kernel, b_specs, b_args = _mm_nt_kernel, [pl.BlockSpec((tn, K), lambda i, j: (b0 + j, 0))], [bt]
    else:
        b1 = (base + tn) // LANES
        step = tn // LANES
        kernel = functools.partial(_mm_nt_shift_kernel, shift=shift)
        b_specs = [pl.BlockSpec((tn, K), lambda i, j: (b0 + j, 0)),
                   pl.BlockSpec((LANES, K), lambda i, j: (b1 + j * step, 0))]
        b_args = [bt, bt]
    return pl.pallas_call(
        kernel, grid=(M // tm, n_rows // tn), in_specs=[a_spec] + b_specs,
        out_specs=pl.BlockSpec((tm, tn), lambda i, j: (i, j)),
        out_shape=jax.ShapeDtypeStruct((M, n_rows), F32),
        compiler_params=_cp(("parallel", "parallel")), name=name)(a, *b_args)


def _mm_res_kloop_kernel(a_ref, b_ref, r_ref, o_ref):
    @pl.when(pl.program_id(2) == 0)
    def _():
        o_ref[...] = r_ref[...]

    o_ref[...] += _bdot(a_ref[...], b_ref[...])


def _mm_res_kloop(a, b, res, name):
    M, K = a.shape
    N = b.shape[1]
    tm = _pick(M, 1408, 16)
    tn = _pick(N, 1024, LANES)
    tk = _pick(K, 1024, LANES)
    return pl.pallas_call(
        _mm_res_kloop_kernel, grid=(M // tm, N // tn, K // tk),
        in_specs=[pl.BlockSpec((tm, tk), lambda i, j, k: (i, k)),
                  pl.BlockSpec((tk, tn), lambda i, j, k: (k, j)),
                  pl.BlockSpec((tm, tn), lambda i, j, k: (i, j))],
        out_specs=pl.BlockSpec((tm, tn), lambda i, j, k: (i, j)),
        out_shape=jax.ShapeDtypeStruct((M, N), F32),
        compiler_params=_cp(("parallel", "parallel", "arbitrary")), name=name)(a, b, res)


def _merge_kernel(yr_ref, yd_ref, wr_ref, wd_ref, gr_ref, gd_ref, o_ref):
    pr = _bdot(yr_ref[...], wr_ref[...])
    pd = _bdot(yd_ref[...], wd_ref[...])
    o_ref[...] = (jax.nn.sigmoid(gr_ref[...]) * pr + jax.nn.sigmoid(gd_ref[...]) * pd).astype(o_ref.dtype)


def _merge(y_r, y_d, w_r, w_d, proj, gate_off):
    M, Kr = y_r.shape
    Kd = y_d.shape[1]
    D = w_r.shape[1]
    tm = _pick(M, 704, 16)
    tn = _pick(D, 512, LANES)
    assert gate_off % tn == 0
    g0 = gate_off // tn
    g1 = (gate_off + D) // tn
    return pl.pallas_call(
        _merge_kernel, grid=(M // tm, D // tn),
        in_specs=[pl.BlockSpec((tm, Kr), lambda i, j: (i, 0)),
                  pl.BlockSpec((tm, Kd), lambda i, j: (i, 0)),
                  pl.BlockSpec((Kr, tn), lambda i, j: (0, j)),
                  pl.BlockSpec((Kd, tn), lambda i, j: (0, j)),
                  pl.BlockSpec((tm, tn), lambda i, j: (i, g0 + j)),
                  pl.BlockSpec((tm, tn), lambda i, j: (i, g1 + j))],
        out_specs=pl.BlockSpec((tm, tn), lambda i, j: (i, j)),
        out_shape=jax.ShapeDtypeStruct((M, D), BF16),
        compiler_params=_cp(("parallel", "parallel")), name="branch_merge")(y_r, y_d, w_r, w_d, proj, proj)


def _ple_kernel(h_ref, wg_ref, pe_ref, wp_ref, x_ref, o_ref):
    gate = jax.nn.sigmoid(_bdot(h_ref[...], wg_ref[...]))
    o_ref[...] = x_ref[...] + gate * _bdot(pe_ref[...], wp_ref[...])


def _ple(h, x, pe, w_gate, w_proj, row_off, rows):
    D = x.shape[1]
    P = pe.shape[1]
    tm = _pick(rows, 1024, 16)
    tn = _pick(D, 512, LANES)
    assert row_off % tm == 0
    r0 = row_off // tm
    return pl.pallas_call(
        _ple_kernel, grid=(rows // tm, D // tn),
        in_specs=[pl.BlockSpec((tm, D), lambda i, j: (r0 + i, 0)),
                  pl.BlockSpec((D, tn), lambda i, j: (0, j)),
                  pl.BlockSpec((tm, P), lambda i, j: (i, 0)),
                  pl.BlockSpec((P, tn), lambda i, j: (0, j)),
                  pl.BlockSpec((tm, tn), lambda i, j: (r0 + i, j))],
        out_specs=pl.BlockSpec((tm, tn), lambda i, j: (i, j)),
        out_shape=jax.ShapeDtypeStruct((rows, D), F32),
        compiler_params=_cp(("parallel", "parallel")), name="ple")(h, w_gate, pe, w_proj, x)


def _head_allsum_wide(x, ones2):
    n = x.shape[1] // LANES
    rows = x.shape[0]
    s = _head_allsum(jnp.concatenate([x[:, i * LANES:(i + 1) * LANES] for i in range(n)], axis=0), ones2)
    return jnp.concatenate([s[i * rows:(i + 1) * rows] for i in range(n)], axis=1)


def _rwkv_kernel(pr_ref, pk_ref, pv_ref, pw_ref, spr_ref, spk_ref, spv_ref, spw_ref,
                 mur_ref, muk_ref, muv_ref, muw_ref, w0_ref, a0_ref, kk_ref, ka_ref, rk_ref,
                 wd_ref, wa_ref, wg_ref, lng_ref, lnb_ref, s0_ref,
                 y_ref, so_ref, last_ref, sbd_ref, prev_ref, *, PB, L, win_w, win_a, win_g):
    c = pl.program_id(2)
    Tb = pr_ref.shape[0]
    W = PB * LANES
    lane = lax.broadcasted_iota(jnp.int32, (L, LANES), 1)
    lane2 = lax.broadcasted_iota(jnp.int32, (2 * L, LANES), 1)
    row = lax.broadcasted_iota(jnp.int32, (L, L), 0)
    col = lax.broadcasted_iota(jnp.int32, (L, L), 1)
    incl = row >= col
    strict = row > col
    tri = jnp.where(incl, 1.0, 0.0).astype(BF16)
    eye = jnp.where(row == col, 1.0, 0.0)
    srow = lax.broadcasted_iota(jnp.int32, (LANES, LANES), 0)
    scol = lax.broadcasted_iota(jnp.int32, (LANES, LANES), 1)
    same_head = (srow >= HEAD) == (scol >= HEAD)
    ones2 = _head_ones()

    @pl.when(c == 0)
    def _():
        sbd_ref[...] = s0_ref[0]
        prev_ref[:, 0:W] = spr_ref[0]
        prev_ref[:, W:2 * W] = spk_ref[0]
        prev_ref[:, 2 * W:3 * W] = spv_ref[0]
        prev_ref[:, 3 * W:] = spw_ref[0]

    first_row = lax.broadcasted_iota(jnp.int32, (Tb, 1), 0) == 0

    def shifted(x_ref, mu_ref, off):
        x = x_ref[...]
        prev = prev_ref[:, off:off + x.shape[1]]
        prev_ref[:, off:off + x.shape[1]] = x[Tb - 1:Tb, :]
        sh = jnp.where(first_row, prev, pltpu.roll(x, 1, axis=0))
        return x + (sh - x) * mu_ref[...]

    r = shifted(pr_ref, mur_ref, 0)
    k = shifted(pk_ref, muk_ref, W)
    v = shifted(pv_ref, muv_ref, 2 * W)
    lo = shifted(pw_ref, muw_ref, 3 * W)
    z = w0_ref[...] + _bdot(jnp.tanh(lo[:, win_w[0]:win_w[1]]), wd_ref[...])
    lw = -math.exp(-0.5) * jax.nn.sigmoid(z)
    alr = jax.nn.sigmoid(a0_ref[...] + _bdot(lo[:, win_a[0]:win_a[1]], wa_ref[...]))
    gate = _bdot(jax.nn.sigmoid(lo[:, win_g[0]:win_g[1]]), wg_ref[...])
    kk = k * kk_ref[...]
    k2 = k * (1.0 + (alr - 1.0) * ka_ref[...])
    kkn = kk * lax.rsqrt(_head_allsum_wide(kk * kk, ones2) + 1e-12)
    bonus_gate = _head_allsum_wide(r * k2 * rk_ref[...], ones2) * v * gate
    if Tb < L:
        pad = lambda x: jnp.concatenate([x, jnp.zeros((L - Tb, W), F32)], axis=0)
        r, k2, v, lw, kkn, alr = pad(r), pad(k2), pad(v), pad(lw), pad(kkn), pad(alr)

    pairs = range(PB)
    heads = [(p, e) for p in pairs for e in range(2)]
    cols = [slice(p * LANES, (p + 1) * LANES) for p in pairs]
    bmm = lambda x, y: lax.dot_general(x.astype(BF16), y.astype(BF16), (((2,), (1,)), ((0,), (0,))),
                                       preferred_element_type=F32)
    bmm_nt = lambda x, y: lax.dot_general(x.astype(BF16), y.astype(BF16), (((2,), (2,)), ((0,), (0,))),
                                          preferred_element_type=F32)
    rows_cat = lambda xs: jnp.concatenate(xs, axis=0)

    w1 = lw.astype(BF16)
    res1 = lw - w1.astype(F32)
    w2 = res1.astype(BF16)
    w3 = (res1 - w2.astype(F32)).astype(BF16)
    lc3 = jnp.dot(tri, jnp.concatenate([w1, w2, w3], axis=1), preferred_element_type=F32)
    lc = lc3[:, :W] + lc3[:, W:2 * W] + lc3[:, 2 * W:]
    lc_last = lc[L - 1:L, :]
    g_in, g_prev, g_inv = jnp.exp(lc), jnp.exp(lc - lw), jnp.exp(-lc)
    ratio = jnp.exp(lc_last - lc)
    g_last = jnp.exp(lc_last)
    b = kkn * alr
    At, Rt, Bt, Kt = -kkn * g_prev, r * g_in, b * g_inv, k2 * g_inv
    Bh, Kh = b * ratio, k2 * ratio

    he = lane >= HEAD
    own = [lane2 < HEAD, lane2 >= HEAD]
    X = [rows_cat([At[:, cs], Rt[:, cs]]) for cs in cols]
    Y = [rows_cat([Bt[:, cs], Kt[:, cs]]) for cs in cols]
    G = bmm_nt(jnp.stack([jnp.where(own[e], X[p], 0.0) for p, e in heads]),
               jnp.stack([Y[p] for p, e in heads]))
    Nab = jnp.where(strict, G[:, :L, :L], 0.0)
    Aak = jnp.where(strict, G[:, :L, L:], 0.0)
    Arb = jnp.where(incl, G[:, L:, :L], 0.0)
    Ark = jnp.where(incl, G[:, L:, L:], 0.0)
    Tm = eye + Nab
    Np = Nab
    lev = 1
    while lev * 2 < L:
        Np = bmm(Np, Np)
        Tm = Tm + bmm(Tm, Np)
        lev *= 2
    AV = bmm(jnp.concatenate([Aak, Ark], axis=1), jnp.stack([v[:, cols[p]] for p, e in heads]))
    TW = bmm(Tm, jnp.concatenate([jnp.stack([At[:, cols[p]] for p, e in heads]), AV[:, :L]], axis=2))
    pick = lambda x, p: jnp.where(he, x[2 * p + 1], x[2 * p])
    W1 = [pick(TW[:, :, :LANES], p) for p in pairs]
    W2 = [pick(TW[:, :, LANES:], p) for p in pairs]
    Op = [pick(AV[:, L:], p) for p in pairs]
    S = sbd_ref[...]
    XS = bmm_nt(jnp.stack([rows_cat([W1[p], Rt[:, cols[p]]]) for p in pairs]), S)
    U = [XS[p, :L] + W2[p] for p in pairs]
    AU = bmm(Arb, jnp.stack([U[p] for p, e in heads]))
    O = [XS[p, L:] + Op[p] + pick(AU, p) for p in pairs]
    upd = bmm(jnp.stack([rows_cat([U[p], v[:, cols[p]]]).T for p in pairs]),
              jnp.stack([rows_cat([Bh[:, cols[p]], Kh[:, cols[p]]]) for p in pairs]))
    sbd_ref[...] = S * jnp.stack([g_last[:, cols[p]] for p in pairs]) + jnp.where(same_head, upd, 0.0)
    Oc = rows_cat(O)
    mu = _head_allsum(Oc, ones2) * (1.0 / HEAD)
    d = Oc - mu
    var = _head_allsum(d * d, ones2) * (1.0 / HEAD)
    nrm = d * lax.rsqrt(var + GN_EPS)
    for p in pairs:
        gn = nrm[p * L:p * L + Tb] * lng_ref[:, cols[p]] + lnb_ref[:, cols[p]]
        y_ref[:, cols[p]] = (gn * gate[:, cols[p]] + bonus_gate[:, cols[p]]).astype(y_ref.dtype)

    @pl.when(c == pl.num_programs(2) - 1)
    def _():
        so_ref[0] = sbd_ref[...]
        last_ref[0, 0] = prev_ref[...]


def _lora_window(start, width, w, base):
    s0 = start // LANES * LANES
    s1 = _round_up(start + width, LANES)
    wp = jnp.zeros((s1 - s0, w.shape[1]), w.dtype).at[start - s0:start - s0 + width].set(w)
    return (s0 - base, s1 - base), wp


def _rwkv(proj, row_off, B, T, SWp, shift_prev, s0_bd, mu, w0, a0, k_k, k_a, r_k, w_decay_up, a_up, g_up,
          lnx_g, lnx_b):
    RW = w0.shape[-1]
    L = SCAN_L
    Tb = min(T, L)
    assert T % Tb == 0 and row_off % Tb == 0
    nc = T // Tb
    r0 = row_off // Tb
    npair = RW // LANES
    PB = math.gcd(npair, SCAN_PAIRS)
    W = PB * LANES
    ng = RW // W
    LW = SWp - 3 * RW
    assert LW % LANES == 0 and (3 * RW) % LW == 0
    lb = 3 * RW // LW
    DL, AL, GL = w_decay_up.shape[0], a_up.shape[0], g_up.shape[0]
    win_w, wd = _lora_window(3 * RW, DL, w_decay_up, 3 * RW)
    win_a, wa = _lora_window(3 * RW + DL, AL, a_up, 3 * RW)
    win_g, wg = _lora_window(3 * RW + DL + AL, GL, g_up, 3 * RW)
    row = lambda x: x.reshape(1, -1)
    seg = lambda s: pl.BlockSpec((Tb, W), lambda b, g, c: (r0 + b * nc + c, s * ng + g))
    seg_prev = lambda s: pl.BlockSpec((1, 1, W), lambda b, g, c: (b, 0, s * ng + g))
    seg_mu = lambda s: pl.BlockSpec((1, W), lambda b, g, c: (0, s * ng + g))
    vec = pl.BlockSpec((1, W), lambda b, g, c: (0, g))
    wblk = lambda w: pl.BlockSpec((w.shape[0], W), lambda b, g, c: (0, g))
    sblk = pl.BlockSpec((1, PB, LANES, LANES), lambda b, g, c: (b, g, 0, 0))
    mu2 = row(mu)
    PW = 3 * W + LW
    y, s_out, last = pl.pallas_call(
        functools.partial(_rwkv_kernel, PB=PB, L=L, win_w=win_w, win_a=win_a, win_g=win_g),
        grid=(B, ng, nc),
        in_specs=[seg(0), seg(1), seg(2), pl.BlockSpec((Tb, LW), lambda b, g, c: (r0 + b * nc + c, lb)),
                  seg_prev(0), seg_prev(1), seg_prev(2), pl.BlockSpec((1, 1, LW), lambda b, g, c: (b, 0, lb)),
                  seg_mu(0), seg_mu(1), seg_mu(2), pl.BlockSpec((1, LW), lambda b, g, c: (0, lb)),
                  vec, vec, vec, vec, vec, wblk(wd), wblk(wa), wblk(wg), vec, vec, sblk],
        out_specs=[pl.BlockSpec((Tb, W), lambda b, g, c: (b * nc + c, g)), sblk,
                   pl.BlockSpec((1, 1, 1, PW), lambda b, g, c: (b, g, 0, 0))],
        out_shape=[jax.ShapeDtypeStruct((B * T, RW), BF16),
                   jax.ShapeDtypeStruct((B, npair, LANES, LANES), F32),
                   jax.ShapeDtypeStruct((B, ng, 1, PW), F32)],
        scratch_shapes=[pltpu.VMEM((PB, LANES, LANES), F32), pltpu.VMEM((1, PW), F32)],
        compiler_params=_cp(("parallel", "parallel", "arbitrary")), name="rwkv",
    )(proj, proj, proj, proj, shift_prev, shift_prev, shift_prev, shift_prev, mu2, mu2, mu2, mu2,
      row(w0), row(a0), row(k_k), row(k_a), row(r_k), wd, wa, wg, row(lnx_g), row(lnx_b), s0_bd)
    segs = [last[:, :, 0, s * W:(s + 1) * W].reshape(B, RW) for s in range(3)]
    shift_out = jnp.concatenate(segs + [last[:, 0, 0, 3 * W:]], axis=1)
    return y, s_out, shift_out


def _to_blockdiag(s):
    B, H = s.shape[:2]
    s = s.reshape(B, H // 2, 2, HEAD, HEAD)
    z = jnp.zeros_like(s[:, :, 0])
    top = jnp.concatenate([s[:, :, 0], z], axis=-1)
    bot = jnp.concatenate([z, s[:, :, 1]], axis=-1)
    return jnp.concatenate([top, bot], axis=-2)


def _from_blockdiag(sbd):
    B, P = sbd.shape[:2]
    return jnp.stack([sbd[:, :, :HEAD, :HEAD], sbd[:, :, HEAD:, HEAD:]], axis=2).reshape(B, 2 * P, HEAD, HEAD)


def _qknorm_kernel(q_ref, k_ref, v_ref, c_ref, s1_ref, s2_ref, qg_ref, kg_ref,
                   qb_o, kf_o, kb_o, vf_o, vb_o):
    cos, s1, s2 = c_ref[...], s1_ref[...], s2_ref[...]
    ones2 = _head_ones()

    def norm_rope(x, gain):
        ms = _head_allsum(x * x, ones2) * (1.0 / HEAD)
        y = x * lax.rsqrt(ms + RMS_EPS) * gain
        half = ROT_DIM // 2
        return y * cos + pltpu.roll(y, LANES - half, axis=1) * s1 + pltpu.roll(y, half, axis=1) * s2

    for c in range(0, q_ref.shape[1], LANES):
        cs = slice(c, c + LANES)
        q = norm_rope(q_ref[:, cs], qg_ref[...])
        k = norm_rope(k_ref[:, cs], kg_ref[...])
        qb_o[:, cs] = (q * (HEAD ** -0.5)).astype(qb_o.dtype)
        kf_o[:, cs] = k
        kb_o[:, cs] = k.astype(kb_o.dtype)
    v = v_ref[...]
    vf_o[...] = v
    vb_o[...] = v.astype(vb_o.dtype)


def _rope_tables(pos):
    half = ROT_DIM // 2
    inv_freq = jnp.power(jnp.float32(ROPE_THETA), -jnp.arange(0, ROT_DIM, 2, dtype=jnp.float32) / ROT_DIM)
    ang = pos.astype(jnp.float32)[:, None] * inv_freq[None, :]
    cos, sin = jnp.cos(ang), jnp.sin(ang)
    T = pos.shape[0]
    pad = jnp.zeros((T, HEAD - ROT_DIM), F32)
    z = jnp.zeros((T, half), F32)
    c = jnp.concatenate([cos, cos, pad + 1.0], axis=1)
    s1 = jnp.concatenate([-sin, z, pad], axis=1)
    s2 = jnp.concatenate([z, sin, pad], axis=1)
    tile = lambda t: jnp.concatenate([t, t], axis=1)
    return tile(c), tile(s1), tile(s2)


def _qknorm(proj, row_off, rows, DW, tables, q_norm_g, k_norm_g):
    tr = _pick(rows, 256, 16)
    cw = _pick(DW, 512, LANES)
    assert row_off % tr == 0
    r0 = row_off // tr
    nc = DW // cw
    gain = lambda g: jnp.concatenate([g, g]).reshape(1, LANES)
    seg = lambda s: pl.BlockSpec((tr, cw), lambda i, j: (r0 + i, s * nc + j))
    tab = pl.BlockSpec((tr, LANES), lambda i, j: (i, 0))
    vec = pl.BlockSpec((1, LANES), lambda i, j: (0, 0))
    out = pl.BlockSpec((tr, cw), lambda i, j: (i, j))
    sds = lambda dt: jax.ShapeDtypeStruct((rows, DW), dt)
    return pl.pallas_call(
        _qknorm_kernel, grid=(rows // tr, nc),
        in_specs=[seg(0), seg(1), seg(2), tab, tab, tab, vec, vec],
        out_specs=[out] * 5,
        out_shape=[sds(BF16), sds(F32), sds(BF16), sds(F32), sds(BF16)],
        compiler_params=_cp(("parallel", "parallel")), name="qk_norm_rope",
    )(proj, proj, proj, *tables, gain(q_norm_g), gain(k_norm_g))


def _lambda(lam_ref, lam_init):
    lam = lam_ref[...]
    return (jnp.exp(jnp.sum(lam[0:1] * lam[1:2], axis=1, keepdims=True))
            - jnp.exp(jnp.sum(lam[2:3] * lam[3:4], axis=1, keepdims=True)) + lam_init)


def _stack_components(q):
    lane = lax.broadcasted_iota(jnp.int32, q.shape, 1)
    zero = jnp.zeros_like(q)
    return jnp.concatenate([jnp.where(lane < HEAD, q, zero), jnp.where(lane >= HEAD, q, zero)], axis=0)


def _softmax_step(qq, k, v, m_ref, l_ref, acc_ref, mask):
    s = lax.dot_general(qq, k, (((2,), (2,)), ((0,), (0,))), preferred_element_type=F32)
    if mask is not None:
        s = jnp.where(mask, s, NEG_INF)
    m_old = m_ref[...]
    m_new = jnp.maximum(m_old, jnp.max(s, axis=2, keepdims=True))
    alpha = jnp.exp(m_old - m_new)
    tk = s.shape[2]
    p = jnp.exp(s - jnp.concatenate([m_new] * pl.cdiv(tk, LANES), axis=2)[:, :, :tk])
    v1 = jnp.concatenate([v, jnp.ones(v.shape, v.dtype)], axis=2)
    pv = lax.dot_general(p.astype(BF16), v1, (((2,), (1,)), ((0,), (0,))), preferred_element_type=F32)
    l_ref[...] = alpha * l_ref[...] + pv[:, :, LANES:]
    acc_ref[...] = alpha * acc_ref[...] + pv[:, :, :LANES]
    m_ref[...] = m_new


def _finish_head(t, m_ref, l_ref, acc_ref, lam, g, lam_init):
    o = acc_ref[...] / l_ref[...]
    o = o[:t] - lam * o[t:]
    ms = jnp.mean(o * o, axis=-1, keepdims=True)
    return o * lax.rsqrt(ms + RMS_EPS) * g * (1.0 - lam_init)


def _attn_prompt_kernel(lam_ref, g_ref, q_ref, k_ref, v_ref, o_ref, m_ref, l_ref, acc_ref, *, lam_init):
    qi = pl.program_id(2)
    tq = q_ref.shape[0]
    nh = q_ref.shape[1] // LANES
    cols = [slice(h * LANES, (h + 1) * LANES) for h in range(nh)]
    qq = jnp.stack([_stack_components(q_ref[:, cs]) for cs in cols])
    m_ref[...] = jnp.full(m_ref.shape, NEG_INF, F32)
    l_ref[...] = jnp.zeros(l_ref.shape, F32)
    acc_ref[...] = jnp.zeros(acc_ref.shape, F32)

    def block(j, mask):
        rows = pl.ds(pl.multiple_of(j * tq, tq), tq)
        _softmax_step(qq, jnp.stack([k_ref[rows, cs] for cs in cols]), jnp.stack([v_ref[rows, cs] for cs in cols]),
                      m_ref, l_ref, acc_ref, mask)

    def body(j, carry):
        block(j, None)
        return carry

    lax.fori_loop(0, qi, body, 0)
    r = lax.broadcasted_iota(jnp.int32, (2 * tq, tq), 0)
    cidx = lax.broadcasted_iota(jnp.int32, (2 * tq, tq), 1)
    r = jnp.where(r >= tq, r - tq, r)
    block(qi, (r // CHUNK) >= (cidx // CHUNK))
    lam = _lambda(lam_ref, lam_init)
    for h, cs in enumerate(cols):
        o_ref[:, cs] = _finish_head(tq, m_ref.at[h], l_ref.at[h], acc_ref.at[h], lam, g_ref[...],
                                    lam_init).astype(o_ref.dtype)


def _attn_prompt(qb, kb, vb, lam_vecs, subln_g, B, T, lam_init):
    DW = qb.shape[1]
    nh = DW // LANES
    hg = math.gcd(nh, ATTN_HEADS)
    tq = _pick(T, 256, CHUNK)
    nq = T // tq
    return pl.pallas_call(
        functools.partial(_attn_prompt_kernel, lam_init=lam_init), grid=(B, nh // hg, nq),
        in_specs=[pl.BlockSpec(lam_vecs.shape, lambda b, h, i: (0, 0)),
                  pl.BlockSpec((1, LANES), lambda b, h, i: (0, 0)),
                  pl.BlockSpec((tq, hg * LANES), lambda b, h, i: (b * nq + i, h)),
                  pl.BlockSpec((T, hg * LANES), lambda b, h, i: (b, h)),
                  pl.BlockSpec((T, hg * LANES), lambda b, h, i: (b, h))],
        out_specs=pl.BlockSpec((tq, hg * LANES), lambda b, h, i: (b * nq + i, h)),
        out_shape=jax.ShapeDtypeStruct((B * T, DW), BF16),
        scratch_shapes=[pltpu.VMEM((hg, 2 * tq, LANES), F32)] * 3,
        compiler_params=_cp(("parallel", "parallel", "arbitrary")), name="attn_prompt",
    )(lam_vecs, subln_g.reshape(1, LANES), qb, kb, vb)


def _attn_sample_kernel(lam_ref, g_ref, q_ref, kc_ref, vc_ref, kn_ref, vn_ref, o_ref, m_ref, l_ref, acc_ref,
                        *, lam_init, past):
    j = pl.program_id(1)
    ts = q_ref.shape[0]
    nh = q_ref.shape[1] // LANES

    @pl.when(j == 0)
    def _():
        m_ref[...] = jnp.full(m_ref.shape, NEG_INF, F32)
        l_ref[...] = jnp.zeros(l_ref.shape, F32)
        acc_ref[...] = jnp.zeros(acc_ref.shape, F32)

    cols = [slice(h * LANES, (h + 1) * LANES) for h in range(nh)]
    tk = kc_ref.shape[0] // nh

    def heads(head_k, head_v, mask):
        qq = jnp.stack([_stack_components(q_ref[:, cs]) for cs in cols])
        _softmax_step(qq, jnp.stack([head_k(h).astype(BF16) for h in range(nh)]),
                      jnp.stack([head_v(h).astype(BF16) for h in range(nh)]), m_ref, l_ref, acc_ref, mask)

    heads(lambda h: kc_ref[pl.ds(h, tk, stride=nh), :], lambda h: vc_ref[pl.ds(h, tk, stride=nh), :], None)

    @pl.when(j == pl.num_programs(1) - 1)
    def _():
        r = lax.broadcasted_iota(jnp.int32, (2 * ts, ts), 0)
        cidx = lax.broadcasted_iota(jnp.int32, (2 * ts, ts), 1)
        r = jnp.where(r >= ts, r - ts, r)
        heads(lambda h: kn_ref[:, cols[h]], lambda h: vn_ref[:, cols[h]],
              ((past + r) // CHUNK) >= ((past + cidx) // CHUNK))
        lam = _lambda(lam_ref, lam_init)
        for h in range(nh):
            cs = slice(h * LANES, (h + 1) * LANES)
            o_ref[:, cs] = _finish_head(ts, m_ref.at[h], l_ref.at[h], acc_ref.at[h], lam, g_ref[...],
                                        lam_init).astype(o_ref.dtype)


def _attn_sample(qb, kb, vb, cache_k, cache_v, lam_vecs, subln_g, B, Ts, lam_init):
    DW = qb.shape[1]
    nh = DW // LANES
    past = cache_k.shape[0] // (B * nh)
    tk = _pick(past, 512, 16)
    nkv = past // tk
    new = pl.BlockSpec((Ts, DW), lambda b, j: (b, 0))
    old = pl.BlockSpec((tk * nh, LANES), lambda b, j: (b * nkv + j, 0))
    return pl.pallas_call(
        functools.partial(_attn_sample_kernel, lam_init=lam_init, past=past), grid=(B, nkv),
        in_specs=[pl.BlockSpec(lam_vecs.shape, lambda b, j: (0, 0)),
                  pl.BlockSpec((1, LANES), lambda b, j: (0, 0)),
                  new, old, old, new, new],
        out_specs=new,
        out_shape=jax.ShapeDtypeStruct((B * Ts, DW), BF16),
        scratch_shapes=[pltpu.VMEM((nh, 2 * Ts, LANES), F32)] * 3,
        compiler_params=_cp(("parallel", "arbitrary")), name="attn_sample",
    )(lam_vecs, subln_g.reshape(1, LANES), qb, cache_k, cache_v, kb, vb)


def _layer(i, x_prompt, x_sample, pe_prompt, pe_sample, cache_k, cache_v, state_rwkv, state_shift, w):
    Bp, Tp, D = x_prompt.shape
    Bs, Ts, _ = x_sample.shape
    Mp, Ms = Bp * Tp, Bs * Ts
    past = cache_k.shape[1]
    RW = w["w0"].shape[-1]
    DW = w["w_branch_diff"].shape[0]
    SW = state_shift.shape[-1]
    SWp = _round_up(SW, SEG_ALIGN)
    w_in = w["w_in"]
    lam_init = 0.8 - 0.6 * math.exp(-0.3 * i)

    x_all = jnp.concatenate([x_prompt.reshape(Mp, D), x_sample.reshape(Ms, D)], axis=0)
    h1 = _rmsnorm(x_all, w["norm1_g"])
    w_t = w_in.T
    proj_r = _mm_nt_rows(h1, w_t, 0, SWp, name="proj_rwkv")
    proj_a = _mm_nt_rows(h1, w_t, SW, w_t.shape[0] - SW, name="proj_attn")

    pad_sw = lambda a: jnp.pad(a, ((0, 0), (0, 0), (0, SWp - SW)))
    mu = jnp.pad(w["tm_mu"], (0, SWp - SW))
    lam_vecs = jnp.stack([w["lam_q1"], w["lam_k1"], w["lam_q2"], w["lam_k2"]])
    rwkv_w = (mu, w["w0"], w["a0"], w["k_k"], w["k_a"], w["r_k"].reshape(-1), w["w_decay_up"], w["a_up"], w["g_up"],
              w["lnx_g"], w["lnx_b"])

    s0_p = jnp.zeros((Bp, RW // LANES, LANES, LANES), F32)
    yr_p, s_p, last_p = _rwkv(proj_r, 0, Bp, Tp, SWp, jnp.zeros((Bp, 1, SWp), F32), s0_p, *rwkv_w)
    tab_p = _rope_tables(jnp.arange(Tp, dtype=jnp.int32))
    tab_p = tuple(jnp.tile(t, (Bp, 1)) for t in tab_p)
    qb_p, kf_p, kb_p, vf_p, vb_p = _qknorm(proj_a, 0, Mp, DW, tab_p, w["q_norm_g"], w["k_norm_g"])
    yd_p = _attn_prompt(qb_p, kb_p, vb_p, lam_vecs, w["subln_g"], Bp, Tp, lam_init)

    yr_s, s_s, last_s = _rwkv(proj_r, Mp, Bs, Ts, SWp, pad_sw(state_shift), _to_blockdiag(state_rwkv), *rwkv_w)
    tab_s = _rope_tables(past + jnp.arange(Ts, dtype=jnp.int32))
    tab_s = tuple(jnp.tile(t, (Bs, 1)) for t in tab_s)
    qb_s, kf_s, kb_s, vf_s, vb_s = _qknorm(proj_a, Mp, Ms, DW, tab_s, w["q_norm_g"], w["k_norm_g"])
    yd_s = _attn_sample(qb_s, kb_s, vb_s, cache_k.reshape(-1, LANES), cache_v.reshape(-1, LANES),
                        lam_vecs, w["subln_g"], Bs, Ts, lam_init)

    y_r = jnp.concatenate([yr_p, yr_s], axis=0)
    y_d = jnp.concatenate([yd_p, yd_s], axis=0)
    merged = _merge(y_r, y_d, w["w_branch_rwkv"], w["w_branch_diff"], proj_a, 3 * DW)
    x1 = _mm_fullk(_mm_res_kernel, merged, w["w_out"], F32, res=x_all, tm_target=704, name="out_proj")
    h2 = _rmsnorm(x1, w["norm2_g"])
    u = _mm_fullk(_mm_relu2_kernel, h2, w["w_up"], BF16, name="ffn_up")
    x2 = _mm_res_kloop(u, w["w_down"], x1, name="ffn_down")
    h3 = _rmsnorm(x2, w["ple_norm_g"])
    y_p = _ple(h3, x2, pe_prompt.reshape(Mp, -1), w["w_ple_gate"], w["w_ple_proj"], 0, Mp)
    y_s = _ple(h3, x2, pe_sample.reshape(Ms, -1), w["w_ple_gate"], w["w_ple_proj"], Mp, Ms)

    nh_d = DW // LANES
    shift_p = last_p[:, :SW].reshape(Bp, 1, SW)
    shift_s = last_s[:, :SW].reshape(Bs, 1, SW)
    return (y_p.reshape(Bp, Tp, D), y_s.reshape(Bs, Ts, D),
            kf_p.reshape(Bp, Tp, nh_d, LANES), vf_p.reshape(Bp, Tp, nh_d, LANES), _from_blockdiag(s_p), shift_p,
            kf_s.reshape(Bs, Ts, nh_d, LANES), vf_s.reshape(Bs, Ts, nh_d, LANES), _from_blockdiag(s_s), shift_s)


def kernel(x_prompt, x_sample, p_prompt, p_sample, cache_k, cache_v, state_rwkv, state_shift, norm1_g, w_in, tm_mu, w_decay_up, w0, a_up, a0, g_up, k_k, k_a, r_k, lnx_g, lnx_b, q_norm_g, k_norm_g, lam_q1, lam_k1, lam_q2, lam_k2, subln_g, w_branch_rwkv, w_branch_diff, w_out, norm2_g, w_up, w_down, ple_norm_g, w_ple_gate, w_ple_proj):
    weights = dict(norm1_g=norm1_g, w_in=w_in, tm_mu=tm_mu, w_decay_up=w_decay_up, w0=w0, a_up=a_up, a0=a0,
                   g_up=g_up, k_k=k_k, k_a=k_a, r_k=r_k, lnx_g=lnx_g, lnx_b=lnx_b, q_norm_g=q_norm_g,
                   k_norm_g=k_norm_g, lam_q1=lam_q1, lam_k1=lam_k1, lam_q2=lam_q2, lam_k2=lam_k2,
                   subln_g=subln_g, w_branch_rwkv=w_branch_rwkv, w_branch_diff=w_branch_diff, w_out=w_out,
                   norm2_g=norm2_g, w_up=w_up, w_down=w_down, ple_norm_g=ple_norm_g, w_ple_gate=w_ple_gate,
                   w_ple_proj=w_ple_proj)
    depth = w_in.shape[0]
    hp, hs = x_prompt, x_sample
    outs = [[] for _ in range(8)]
    for i in range(depth):
        res = _layer(i, hp, hs, p_prompt[i], p_sample[i], cache_k[i], cache_v[i], state_rwkv[i], state_shift[i],
                     {name: val[i] for name, val in weights.items()})
        hp, hs = res[0], res[1]
        for acc, val in zip(outs, res[2:]):
            acc.append(val)
    return (hp, hs) + tuple(jnp.stack(acc, 0) for acc in outs)
```

```python
import functools
import math

import jax
import jax.numpy as jnp
from jax import lax
from jax.experimental import pallas as pl
from jax.experimental.pallas import tpu as pltpu

F32 = jnp.float32
BF16 = jnp.bfloat16

LANES = 128
HEAD = 64
RMS_EPS = 1e-6
GN_EPS = HEAD * 1e-5
NEG_INF = -1e30
CHUNK = 64
ROT_DIM = HEAD // 4
ROPE_THETA = 500000.0
SCAN_L = 128
SCAN_PAIRS = 8
ATTN_HEADS = 4
SEG_ALIGN = 512
VMEM_LIMIT = 56 * 1024 * 1024
NT = (((1,), (1,)), ((), ()))


def _round_up(n, m):
    return (n + m - 1) // m * m


def _pick(n, target, mult):
    best = None
    for d in range(mult, min(n, target) + 1, mult):
        if n % d == 0:
            best = d
    assert best is not None, (n, target, mult)
    return best


def _cp(sem):
    return pltpu.CompilerParams(dimension_semantics=sem, vmem_limit_bytes=VMEM_LIMIT)


def _bdot(a, b):
    return jnp.dot(a.astype(BF16), b.astype(BF16), preferred_element_type=F32)


def _bdot_nt(a, b):
    return lax.dot_general(a.astype(BF16), b.astype(BF16), NT, preferred_element_type=F32)


def _head_ones():
    r = lax.broadcasted_iota(jnp.int32, (2 * LANES, LANES), 0)
    c = lax.broadcasted_iota(jnp.int32, (2 * LANES, LANES), 1)
    r = jnp.where(r >= LANES, r - LANES, r)
    return jnp.where((r >= HEAD) == (c >= HEAD), 1.0, 0.0).astype(BF16)


def _head_allsum(x, ones2):
    hi = x.astype(BF16)
    lo = (x - hi.astype(F32)).astype(BF16)
    return jnp.dot(jnp.concatenate([hi, lo], axis=1), ones2, preferred_element_type=F32)


def _rmsnorm_kernel(x_ref, g_ref, o_ref):
    x = x_ref[...]
    ms = jnp.mean(x * x, axis=-1, keepdims=True)
    o_ref[...] = (x * lax.rsqrt(ms + RMS_EPS) * g_ref[...]).astype(o_ref.dtype)


def _rmsnorm(x, g):
    M, D = x.shape
    tr = _pick(M, 256, 16)
    return pl.pallas_call(
        _rmsnorm_kernel, grid=(M // tr,),
        in_specs=[pl.BlockSpec((tr, D), lambda i: (i, 0)), pl.BlockSpec((1, D), lambda i: (0, 0))],
        out_specs=pl.BlockSpec((tr, D), lambda i: (i, 0)),
        out_shape=jax.ShapeDtypeStruct((M, D), BF16),
        compiler_params=_cp(("parallel",)), name="rmsnorm")(x, g.reshape(1, D))


def _mm_relu2_kernel(a_ref, b_ref, o_ref):
    p = jnp.maximum(_bdot(a_ref[...], b_ref[...]), 0.0)
    o_ref[...] = (p * p).astype(o_ref.dtype)


def _mm_res_kernel(a_ref, b_ref, r_ref, o_ref):
    o_ref[...] = r_ref[...] + _bdot(a_ref[...], b_ref[...])


def _mm_fullk(kernel, a, b, out_dtype, res=None, tm_target=1408, tn_target=512, name="mm"):
    M, K = a.shape
    N = b.shape[1]
    tm = _pick(M, tm_target, 16)
    tn = _pick(N, tn_target, LANES)
    in_specs = [pl.BlockSpec((tm, K), lambda i, j: (i, 0)), pl.BlockSpec((K, tn), lambda i, j: (0, j))]
    args = [a, b]
    if res is not None:
        in_specs.append(pl.BlockSpec((tm, tn), lambda i, j: (i, j)))
        args.append(res)
    return pl.pallas_call(
        kernel, grid=(M // tm, N // tn), in_specs=in_specs,
        out_specs=pl.BlockSpec((tm, tn), lambda i, j: (i, j)),
        out_shape=jax.ShapeDtypeStruct((M, N), out_dtype),
        compiler_params=_cp(("parallel", "parallel")), name=name)(*args)


def _mm_nt_kernel(a_ref, bt_ref, o_ref):
    o_ref[...] = _bdot_nt(a_ref[...], bt_ref[...])


def _mm_nt_rows(a, bt, row0, n_rows, name):
    M, K = a.shape
    tm = _pick(M, 1408, 16)
    tn = _pick(n_rows, 512, LANES)
    assert row0 % 8 == 0
    return pl.pallas_call(
        _mm_nt_kernel, grid=(M // tm, n_rows // tn),
        in_specs=[pl.BlockSpec((tm, K), lambda i, j: (i, 0)),
                  pl.BlockSpec((pl.Element(tn), pl.Element(K)),
                               lambda i, j: ((row0 // 8 + j * (tn // 8)) * 8, 0))],
        out_specs=pl.BlockSpec((tm, tn), lambda i, j: (i, j)),
        out_shape=jax.ShapeDtypeStruct((M, n_rows), F32),
        compiler_params=_cp(("parallel", "parallel")), name=name)(a, bt)


def _mm_res_kloop_kernel(a_ref, b_ref, r_ref, o_ref):
    @pl.when(pl.program_id(2) == 0)
    def _():
        o_ref[...] = r_ref[...]

    o_ref[...] += _bdot(a_ref[...], b_ref[...])


def _mm_res_kloop(a, b, res, name):
    M, K = a.shape
    N = b.shape[1]
    tm = _pick(M, 1408, 16)
    tn = _pick(N, 1024, LANES)
    tk = _pick(K, 1024, LANES)
    return pl.pallas_call(
        _mm_res_kloop_kernel, grid=(M // tm, N // tn, K // tk),
        in_specs=[pl.BlockSpec((tm, tk), lambda i, j, k: (i, k)),
                  pl.BlockSpec((tk, tn), lambda i, j, k: (k, j)),
                  pl.BlockSpec((tm, tn), lambda i, j, k: (i, j))],
        out_specs=pl.BlockSpec((tm, tn), lambda i, j, k: (i, j)),
        out_shape=jax.ShapeDtypeStruct((M, N), F32),
        compiler_params=_cp(("parallel", "parallel", "arbitrary")), name=name)(a, b, res)


def _merge_kernel(yr_ref, yd_ref, wr_ref, wd_ref, gr_ref, gd_ref, o_ref):
    pr = _bdot(yr_ref[...], wr_ref[...])
    pd = _bdot(yd_ref[...], wd_ref[...])
    o_ref[...] = (jax.nn.sigmoid(gr_ref[...]) * pr + jax.nn.sigmoid(gd_ref[...]) * pd).astype(o_ref.dtype)


def _merge(y_r, y_d, w_r, w_d, proj, gate_off):
    M, Kr = y_r.shape
    Kd = y_d.shape[1]
    D = w_r.shape[1]
    tm = _pick(M, 1408, 16)
    tn = _pick(D, 256, LANES)
    assert gate_off % tn == 0
    g0 = gate_off // tn
    g1 = (gate_off + D) // tn
    return pl.pallas_call(
        _merge_kernel, grid=(M // tm, D // tn),
        in_specs=[pl.BlockSpec((tm, Kr), lambda i, j: (i, 0)),
                  pl.BlockSpec((tm, Kd), lambda i, j: (i, 0)),
                  pl.BlockSpec((Kr, tn), lambda i, j: (0, j)),
                  pl.BlockSpec((Kd, tn), lambda i, j: (0, j)),
                  pl.BlockSpec((tm, tn), lambda i, j: (i, g0 + j)),
                  pl.BlockSpec((tm, tn), lambda i, j: (i, g1 + j))],
        out_specs=pl.BlockSpec((tm, tn), lambda i, j: (i, j)),
        out_shape=jax.ShapeDtypeStruct((M, D), BF16),
        compiler_params=_cp(("parallel", "parallel")), name="branch_merge")(y_r, y_d, w_r, w_d, proj, proj)


def _ple_kernel(h_ref, wg_ref, pe_ref, wp_ref, x_ref, o_ref):
    gate = jax.nn.sigmoid(_bdot(h_ref[...], wg_ref[...]))
    o_ref[...] = x_ref[...] + gate * _bdot(pe_ref[...], wp_ref[...])


def _ple(h, x, pe, w_gate, w_proj, row_off, rows):
    D = x.shape[1]
    P = pe.shape[1]
    tm = _pick(rows, 1024, 16)
    tn = _pick(D, 512, LANES)
    assert row_off % tm == 0
    r0 = row_off // tm
    return pl.pallas_call(
        _ple_kernel, grid=(rows // tm, D // tn),
        in_specs=[pl.BlockSpec((tm, D), lambda i, j: (r0 + i, 0)),
                  pl.BlockSpec((D, tn), lambda i, j: (0, j)),
                  pl.BlockSpec((tm, P), lambda i, j: (i, 0)),
                  pl.BlockSpec((P, tn), lambda i, j: (0, j)),
                  pl.BlockSpec((tm, tn), lambda i, j: (r0 + i, j))],
        out_specs=pl.BlockSpec((tm, tn), lambda i, j: (i, j)),
        out_shape=jax.ShapeDtypeStruct((rows, D), F32),
        compiler_params=_cp(("parallel", "parallel")), name="ple")(h, w_gate, pe, w_proj, x)


def _head_allsum_wide(x, ones2):
    n = x.shape[1] // LANES
    rows = x.shape[0]
    s = _head_allsum(jnp.concatenate([x[:, i * LANES:(i + 1) * LANES] for i in range(n)], axis=0), ones2)
    return jnp.concatenate([s[i * rows:(i + 1) * rows] for i in range(n)], axis=1)


def _rwkv_kernel(pr_ref, pk_ref, pv_ref, pw_ref, spr_ref, spk_ref, spv_ref, spw_ref,
                 mur_ref, muk_ref, muv_ref, muw_ref, w0_ref, a0_ref, kk_ref, ka_ref, rk_ref,
                 wd_ref, wa_ref, wg_ref, lng_ref, lnb_ref, s0_ref,
                 y_ref, so_ref, last_ref, sbd_ref, prev_ref, *, PB, L, win_w, win_a, win_g):
    c = pl.program_id(2)
    Tb = pr_ref.shape[0]
    W = PB * LANES
    lane = lax.broadcasted_iota(jnp.int32, (L, LANES), 1)
    lane2 = lax.broadcasted_iota(jnp.int32, (2 * L, LANES), 1)
    row = lax.broadcasted_iota(jnp.int32, (L, L), 0)
    col = lax.broadcasted_iota(jnp.int32, (L, L), 1)
    incl = row >= col
    strict = row > col
    tri = jnp.where(incl, 1.0, 0.0).astype(BF16)
    eye = jnp.where(row == col, 1.0, 0.0)
    srow = lax.broadcasted_iota(jnp.int32, (LANES, LANES), 0)
    scol = lax.broadcasted_iota(jnp.int32, (LANES, LANES), 1)
    same_head = (srow >= HEAD) == (scol >= HEAD)
    ones2 = _head_ones()

    @pl.when(c == 0)
    def _():
        sbd_ref[...] = s0_ref[0]
        prev_ref[:, 0:W] = spr_ref[0]
        prev_ref[:, W:2 * W] = spk_ref[0]
        prev_ref[:, 2 * W:3 * W] = spv_ref[0]
        prev_ref[:, 3 * W:] = spw_ref[0]

    first_row = lax.broadcasted_iota(jnp.int32, (Tb, 1), 0) == 0

    def shifted(x_ref, mu_ref, off):
        x = x_ref[...]
        prev = prev_ref[:, off:off + x.shape[1]]
        prev_ref[:, off:off + x.shape[1]] = x[Tb - 1:Tb, :]
        sh = jnp.where(first_row, prev, pltpu.roll(x, 1, axis=0))
        return x + (sh - x) * mu_ref[...]

    r = shifted(pr_ref, mur_ref, 0)
    k = shifted(pk_ref, muk_ref, W)
    v = shifted(pv_ref, muv_ref, 2 * W)
    lo = shifted(pw_ref, muw_ref, 3 * W)
    z = w0_ref[...] + _bdot(jnp.tanh(lo[:, win_w[0]:win_w[1]]), wd_ref[...])
    lw = -math.exp(-0.5) * jax.nn.sigmoid(z)
    alr = jax.nn.sigmoid(a0_ref[...] + _bdot(lo[:, win_a[0]:win_a[1]], wa_ref[...]))
    gate = _bdot(jax.nn.sigmoid(lo[:, win_g[0]:win_g[1]]), wg_ref[...])
    kk = k * kk_ref[...]
    k2 = k * (1.0 + (alr - 1.0) * ka_ref[...])
    kkn = kk * lax.rsqrt(_head_allsum_wide(kk * kk, ones2) + 1e-12)
    bonus_gate = _head_allsum_wide(r * k2 * rk_ref[...], ones2) * v * gate
    if Tb < L:
        pad = lambda x: jnp.concatenate([x, jnp.zeros((L - Tb, W), F32)], axis=0)
        r, k2, v, lw, kkn, alr = pad(r), pad(k2), pad(v), pad(lw), pad(kkn), pad(alr)

    pairs = range(PB)
    heads = [(p, e) for p in pairs for e in range(2)]
    cols = [slice(p * LANES, (p + 1) * LANES) for p in pairs]
    bmm = lambda x, y: lax.dot_general(x.astype(BF16), y.astype(BF16), (((2,), (1,)), ((0,), (0,))),
                                       preferred_element_type=F32)
    bmm_nt = lambda x, y: lax.dot_general(x.astype(BF16), y.astype(BF16), (((2,), (2,)), ((0,), (0,))),
                                          preferred_element_type=F32)
    rows_cat = lambda xs: jnp.concatenate(xs, axis=0)

    w1 = lw.astype(BF16)
    res1 = lw - w1.astype(F32)
    w2 = res1.astype(BF16)
    w3 = (res1 - w2.astype(F32)).astype(BF16)
    lc3 = jnp.dot(tri, jnp.concatenate([w1, w2, w3], axis=1), preferred_element_type=F32)
    lc = lc3[:, :W] + lc3[:, W:2 * W] + lc3[:, 2 * W:]
    lc_last = lc[L - 1:L, :]
    g_in, g_prev, g_inv = jnp.exp(lc), jnp.exp(lc - lw), jnp.exp(-lc)
    ratio = jnp.exp(lc_last - lc)
    g_last = jnp.exp(lc_last)
    b = kkn * alr
    At, Rt, Bt, Kt = -kkn * g_prev, r * g_in, b * g_inv, k2 * g_inv
    Bh, Kh = b * ratio, k2 * ratio

    he = lane >= HEAD
    own = [lane2 < HEAD, lane2 >= HEAD]
    X = [rows_cat([At[:, cs], Rt[:, cs]]) for cs in cols]
    Y = [rows_cat([Bt[:, cs], Kt[:, cs]]) for cs in cols]
    G = bmm_nt(jnp.stack([jnp.where(own[e], X[p], 0.0) for p, e in heads]),
               jnp.stack([Y[p] for p, e in heads]))
    Nab = jnp.where(strict, G[:, :L, :L], 0.0)
    Aak = jnp.where(strict, G[:, :L, L:], 0.0)
    Arb = jnp.where(incl, G[:, L:, :L], 0.0)
    Ark = jnp.where(incl, G[:, L:, L:], 0.0)
    Tm = eye + Nab
    Np = Nab
    lev = 1
    while lev * 2 < L:
        Np = bmm(Np, Np)
        Tm = Tm + bmm(Tm, Np)
        lev *= 2
    AV = bmm(jnp.concatenate([Aak, Ark], axis=1), jnp.stack([v[:, cols[p]] for p, e in heads]))
    TW = bmm(Tm, jnp.concatenate([jnp.stack([At[:, cols[p]] for p, e in heads]), AV[:, :L]], axis=2))
    pick = lambda x, p: jnp.where(he, x[2 * p + 1], x[2 * p])
    W1 = [pick(TW[:, :, :LANES], p) for p in pairs]
    W2 = [pick(TW[:, :, LANES:], p) for p in pairs]
    Op = [pick(AV[:, L:], p) for p in pairs]
    S = sbd_ref[...]
    XS = bmm_nt(jnp.stack([rows_cat([W1[p], Rt[:, cols[p]]]) for p in pairs]), S)
    U = [XS[p, :L] + W2[p] for p in pairs]
    AU = bmm(Arb, jnp.stack([U[p] for p, e in heads]))
    O = [XS[p, L:] + Op[p] + pick(AU, p) for p in pairs]
    upd = bmm(jnp.stack([rows_cat([U[p], v[:, cols[p]]]).T for p in pairs]),
              jnp.stack([rows_cat([Bh[:, cols[p]], Kh[:, cols[p]]]) for p in pairs]))
    sbd_ref[...] = S * jnp.stack([g_last[:, cols[p]] for p in pairs]) + jnp.where(same_head, upd, 0.0)
    Oc = rows_cat(O)
    mu = _head_allsum(Oc, ones2) * (1.0 / HEAD)
    d = Oc - mu
    var = _head_allsum(d * d, ones2) * (1.0 / HEAD)
    nrm = d * lax.rsqrt(var + GN_EPS)
    for p in pairs:
        gn = nrm[p * L:p * L + Tb] * lng_ref[:, cols[p]] + lnb_ref[:, cols[p]]
        y_ref[:, cols[p]] = (gn * gate[:, cols[p]] + bonus_gate[:, cols[p]]).astype(y_ref.dtype)

    @pl.when(c == pl.num_programs(2) - 1)
    def _():
        so_ref[0] = sbd_ref[...]
        last_ref[0, 0] = prev_ref[...]


def _lora_window(start, width, w, base):
    s0 = start // LANES * LANES
    s1 = _round_up(start + width, LANES)
    wp = jnp.zeros((s1 - s0, w.shape[1]), w.dtype).at[start - s0:start - s0 + width].set(w)
    return (s0 - base, s1 - base), wp


def _rwkv(proj, row_off, B, T, SWp, shift_prev, s0_bd, mu, w0, a0, k_k, k_a, r_k, w_decay_up, a_up, g_up,
          lnx_g, lnx_b):
    RW = w0.shape[-1]
    L = SCAN_L
    Tb = min(T, L)
    assert T % Tb == 0 and row_off % Tb == 0
    nc = T // Tb
    r0 = row_off // Tb
    npair = RW // LANES
    PB = math.gcd(npair, SCAN_PAIRS)
    W = PB * LANES
    ng = RW // W
    LW = SWp - 3 * RW
    assert LW % LANES == 0 and (3 * RW) % LW == 0
    lb = 3 * RW // LW
    DL, AL, GL = w_decay_up.shape[0], a_up.shape[0], g_up.shape[0]
    win_w, wd = _lora_window(3 * RW, DL, w_decay_up, 3 * RW)
    win_a, wa = _lora_window(3 * RW + DL, AL, a_up, 3 * RW)
    win_g, wg = _lora_window(3 * RW + DL + AL, GL, g_up, 3 * RW)
    row = lambda x: x.reshape(1, -1)
    seg = lambda s: pl.BlockSpec((Tb, W), lambda b, g, c: (r0 + b * nc + c, s * ng + g))
    seg_prev = lambda s: pl.BlockSpec((1, 1, W), lambda b, g, c: (b, 0, s * ng + g))
    seg_mu = lambda s: pl.BlockSpec((1, W), lambda b, g, c: (0, s * ng + g))
    vec = pl.BlockSpec((1, W), lambda b, g, c: (0, g))
    wblk = lambda w: pl.BlockSpec((w.shape[0], W), lambda b, g, c: (0, g))
    sblk = pl.BlockSpec((1, PB, LANES, LANES), lambda b, g, c: (b, g, 0, 0))
    mu2 = row(mu)
    PW = 3 * W + LW
    y, s_out, last = pl.pallas_call(
        functools.partial(_rwkv_kernel, PB=PB, L=L, win_w=win_w, win_a=win_a, win_g=win_g),
        grid=(B, ng, nc),
        in_specs=[seg(0), seg(1), seg(2), pl.BlockSpec((Tb, LW), lambda b, g, c: (r0 + b * nc + c, lb)),
                  seg_prev(0), seg_prev(1), seg_prev(2), pl.BlockSpec((1, 1, LW), lambda b, g, c: (b, 0, lb)),
                  seg_mu(0), seg_mu(1), seg_mu(2), pl.BlockSpec((1, LW), lambda b, g, c: (0, lb)),
                  vec, vec, vec, vec, vec, wblk(wd), wblk(wa), wblk(wg), vec, vec, sblk],
        out_specs=[pl.BlockSpec((Tb, W), lambda b, g, c: (b * nc + c, g)), sblk,
                   pl.BlockSpec((1, 1, 1, PW), lambda b, g, c: (b, g, 0, 0))],
        out_shape=[jax.ShapeDtypeStruct((B * T, RW), BF16),
                   jax.ShapeDtypeStruct((B, npair, LANES, LANES), F32),
                   jax.ShapeDtypeStruct((B, ng, 1, PW), F32)],
        scratch_shapes=[pltpu.VMEM((PB, LANES, LANES), F32), pltpu.VMEM((1, PW), F32)],
        compiler_params=_cp(("parallel", "parallel", "arbitrary")), name="rwkv",
    )(proj, proj, proj, proj, shift_prev, shift_prev, shift_prev, shift_prev, mu2, mu2, mu2, mu2,
      row(w0), row(a0), row(k_k), row(k_a), row(r_k), wd, wa, wg, row(lnx_g), row(lnx_b), s0_bd)
    segs = [last[:, :, 0, s * W:(s + 1) * W].reshape(B, RW) for s in range(3)]
    shift_out = jnp.concatenate(segs + [last[:, 0, 0, 3 * W:]], axis=1)
    return y, s_out, shift_out


def _to_blockdiag(s):
    B, H = s.shape[:2]
    s = s.reshape(B, H // 2, 2, HEAD, HEAD)
    z = jnp.zeros_like(s[:, :, 0])
    top = jnp.concatenate([s[:, :, 0], z], axis=-1)
    bot = jnp.concatenate([z, s[:, :, 1]], axis=-1)
    return jnp.concatenate([top, bot], axis=-2)


def _from_blockdiag(sbd):
    B, P = sbd.shape[:2]
    return jnp.stack([sbd[:, :, :HEAD, :HEAD], sbd[:, :, HEAD:, HEAD:]], axis=2).reshape(B, 2 * P, HEAD, HEAD)


def _qknorm_kernel(q_ref, k_ref, v_ref, c_ref, s1_ref, s2_ref, qg_ref, kg_ref,
                   qb_o, kf_o, kb_o, vf_o, vb_o):
    cos, s1, s2 = c_ref[...], s1_ref[...], s2_ref[...]
    ones2 = _head_ones()

    def norm_rope(x, gain):
        ms = _head_allsum(x * x, ones2) * (1.0 / HEAD)
        y = x * lax.rsqrt(ms + RMS_EPS) * gain
        half = ROT_DIM // 2
        return y * cos + pltpu.roll(y, LANES - half, axis=1) * s1 + pltpu.roll(y, half, axis=1) * s2

    for c in range(0, q_ref.shape[1], LANES):
        cs = slice(c, c + LANES)
        q = norm_rope(q_ref[:, cs], qg_ref[...])
        k = norm_rope(k_ref[:, cs], kg_ref[...])
        qb_o[:, cs] = (q * (HEAD ** -0.5)).astype(qb_o.dtype)
        kf_o[:, cs] = k
        kb_o[:, cs] = k.astype(kb_o.dtype)
    v = v_ref[...]
    vf_o[...] = v
    vb_o[...] = v.astype(vb_o.dtype)


def _rope_tables(pos):
    half = ROT_DIM // 2
    inv_freq = jnp.power(jnp.float32(ROPE_THETA), -jnp.arange(0, ROT_DIM, 2, dtype=jnp.float32) / ROT_DIM)
    ang = pos.astype(jnp.float32)[:, None] * inv_freq[None, :]
    cos, sin = jnp.cos(ang), jnp.sin(ang)
    T = pos.shape[0]
    pad = jnp.zeros((T, HEAD - ROT_DIM), F32)
    z = jnp.zeros((T, half), F32)
    c = jnp.concatenate([cos, cos, pad + 1.0], axis=1)
    s1 = jnp.concatenate([-sin, z, pad], axis=1)
    s2 = jnp.concatenate([z, sin, pad], axis=1)
    tile = lambda t: jnp.concatenate([t, t], axis=1)
    return tile(c), tile(s1), tile(s2)


def _qknorm(proj, row_off, rows, DW, tables, q_norm_g, k_norm_g):
    tr = _pick(rows, 256, 16)
    cw = _pick(DW, 512, LANES)
    assert row_off % tr == 0
    r0 = row_off // tr
    nc = DW // cw
    gain = lambda g: jnp.concatenate([g, g]).reshape(1, LANES)
    seg = lambda s: pl.BlockSpec((tr, cw), lambda i, j: (r0 + i, s * nc + j))
    tab = pl.BlockSpec((tr, LANES), lambda i, j: (i, 0))
    vec = pl.BlockSpec((1, LANES), lambda i, j: (0, 0))
    out = pl.BlockSpec((tr, cw), lambda i, j: (i, j))
    sds = lambda dt: jax.ShapeDtypeStruct((rows, DW), dt)
    return pl.pallas_call(
        _qknorm_kernel, grid=(rows // tr, nc),
        in_specs=[seg(0), seg(1), seg(2), tab, tab, tab, vec, vec],
        out_specs=[out] * 5,
        out_shape=[sds(BF16), sds(F32), sds(BF16), sds(F32), sds(BF16)],
        compiler_params=_cp(("parallel", "parallel")), name="qk_norm_rope",
    )(proj, proj, proj, *tables, gain(q_norm_g), gain(k_norm_g))


def _lambda(lam_ref, lam_init):
    lam = lam_ref[...]
    return (jnp.exp(jnp.sum(lam[0:1] * lam[1:2], axis=1, keepdims=True))
            - jnp.exp(jnp.sum(lam[2:3] * lam[3:4], axis=1, keepdims=True)) + lam_init)


def _stack_components(q):
    lane = lax.broadcasted_iota(jnp.int32, q.shape, 1)
    zero = jnp.zeros_like(q)
    return jnp.concatenate([jnp.where(lane < HEAD, q, zero), jnp.where(lane >= HEAD, q, zero)], axis=0)


def _softmax_step(qq, k, v, m_ref, l_ref, acc_ref, mask):
    s = lax.dot_general(qq, k, (((2,), (2,)), ((0,), (0,))), preferred_element_type=F32)
    if mask is not None:
        s = jnp.where(mask, s, NEG_INF)
    m_old = m_ref[...]
    m_new = jnp.maximum(m_old, jnp.max(s, axis=2, keepdims=True))
    alpha = jnp.exp(m_old - m_new)
    tk = s.shape[2]
    p = jnp.exp(s - jnp.concatenate([m_new] * pl.cdiv(tk, LANES), axis=2)[:, :, :tk])
    v1 = jnp.concatenate([v, jnp.ones(v.shape, v.dtype)], axis=2)
    pv = lax.dot_general(p.astype(BF16), v1, (((2,), (1,)), ((0,), (0,))), preferred_element_type=F32)
    l_ref[...] = alpha * l_ref[...] + pv[:, :, LANES:]
    acc_ref[...] = alpha * acc_ref[...] + pv[:, :, :LANES]
    m_ref[...] = m_new


def _finish_head(t, m_ref, l_ref, acc_ref, lam, g, lam_init):
    o = acc_ref[...] / l_ref[...]
    o = o[:t] - lam * o[t:]
    ms = jnp.mean(o * o, axis=-1, keepdims=True)
    return o * lax.rsqrt(ms + RMS_EPS) * g * (1.0 - lam_init)


def _attn_prompt_kernel(lam_ref, g_ref, q_ref, k_ref, v_ref, o_ref, m_ref, l_ref, acc_ref, *, lam_init):
    qi = pl.program_id(2)
    tq = q_ref.shape[0]
    nh = q_ref.shape[1] // LANES
    cols = [slice(h * LANES, (h + 1) * LANES) for h in range(nh)]
    qq = jnp.stack([_stack_components(q_ref[:, cs]) for cs in cols])
    m_ref[...] = jnp.full(m_ref.shape, NEG_INF, F32)
    l_ref[...] = jnp.zeros(l_ref.shape, F32)
    acc_ref[...] = jnp.zeros(acc_ref.shape, F32)

    def block(j, mask):
        rows = pl.ds(pl.multiple_of(j * tq, tq), tq)
        _softmax_step(qq, jnp.stack([k_ref[rows, cs] for cs in cols]), jnp.stack([v_ref[rows, cs] for cs in cols]),
                      m_ref, l_ref, acc_ref, mask)

    def body(j, carry):
        block(j, None)
        return carry

    lax.fori_loop(0, qi, body, 0)
    r = lax.broadcasted_iota(jnp.int32, (2 * tq, tq), 0)
    cidx = lax.broadcasted_iota(jnp.int32, (2 * tq, tq), 1)
    r = jnp.where(r >= tq, r - tq, r)
    block(qi, (r // CHUNK) >= (cidx // CHUNK))
    lam = _lambda(lam_ref, lam_init)
    for h, cs in enumerate(cols):
        o_ref[:, cs] = _finish_head(tq, m_ref.at[h], l_ref.at[h], acc_ref.at[h], lam, g_ref[...],
                                    lam_init).astype(o_ref.dtype)


def _attn_prompt(qb, kb, vb, lam_vecs, subln_g, B, T, lam_init):
    DW = qb.shape[1]
    nh = DW // LANES
    hg = math.gcd(nh, ATTN_HEADS)
    tq = _pick(T, 256, CHUNK)
    nq = T // tq
    return pl.pallas_call(
        functools.partial(_attn_prompt_kernel, lam_init=lam_init), grid=(B, nh // hg, nq),
        in_specs=[pl.BlockSpec(lam_vecs.shape, lambda b, h, i: (0, 0)),
                  pl.BlockSpec((1, LANES), lambda b, h, i: (0, 0)),
                  pl.BlockSpec((tq, hg * LANES), lambda b, h, i: (b * nq + i, h)),
                  pl.BlockSpec((T, hg * LANES), lambda b, h, i: (b, h)),
                  pl.BlockSpec((T, hg * LANES), lambda b, h, i: (b, h))],
        out_specs=pl.BlockSpec((tq, hg * LANES), lambda b, h, i: (b * nq + i, h)),
        out_shape=jax.ShapeDtypeStruct((B * T, DW), BF16),
        scratch_shapes=[pltpu.VMEM((hg, 2 * tq, LANES), F32)] * 3,
        compiler_params=_cp(("parallel", "parallel", "arbitrary")), name="attn_prompt",
    )(lam_vecs, subln_g.reshape(1, LANES), qb, kb, vb)


def _attn_sample_kernel(lam_ref, g_ref, q_ref, kc_ref, vc_ref, kn_ref, vn_ref, o_ref, m_ref, l_ref, acc_ref,
                        *, lam_init, past):
    j = pl.program_id(1)
    ts = q_ref.shape[0]
    nh = q_ref.shape[1] // LANES

    @pl.when(j == 0)
    def _():
        m_ref[...] = jnp.full(m_ref.shape, NEG_INF, F32)
        l_ref[...] = jnp.zeros(l_ref.shape, F32)
        acc_ref[...] = jnp.zeros(acc_ref.shape, F32)

    cols = [slice(h * LANES, (h + 1) * LANES) for h in range(nh)]
    tk = kc_ref.shape[0] // nh

    def heads(head_k, head_v, mask):
        qq = jnp.stack([_stack_components(q_ref[:, cs]) for cs in cols])
        _softmax_step(qq, jnp.stack([head_k(h).astype(BF16) for h in range(nh)]),
                      jnp.stack([head_v(h).astype(BF16) for h in range(nh)]), m_ref, l_ref, acc_ref, mask)

    heads(lambda h: kc_ref[pl.ds(h, tk, stride=nh), :], lambda h: vc_ref[pl.ds(h, tk, stride=nh), :], None)

    @pl.when(j == pl.num_programs(1) - 1)
    def _():
        r = lax.broadcasted_iota(jnp.int32, (2 * ts, ts), 0)
        cidx = lax.broadcasted_iota(jnp.int32, (2 * ts, ts), 1)
        r = jnp.where(r >= ts, r - ts, r)
        heads(lambda h: kn_ref[:, cols[h]], lambda h: vn_ref[:, cols[h]],
              ((past + r) // CHUNK) >= ((past + cidx) // CHUNK))
        lam = _lambda(lam_ref, lam_init)
        for h in range(nh):
            cs = slice(h * LANES, (h + 1) * LANES)
            o_ref[:, cs] = _finish_head(ts, m_ref.at[h], l_ref.at[h], acc_ref.at[h], lam, g_ref[...],
                                        lam_init).astype(o_ref.dtype)


def _attn_sample(qb, kb, vb, cache_k, cache_v, lam_vecs, subln_g, B, Ts, lam_init):
    DW = qb.shape[1]
    nh = DW // LANES
    past = cache_k.shape[0] // (B * nh)
    tk = _pick(past, 512, 16)
    nkv = past // tk
    new = pl.BlockSpec((Ts, DW), lambda b, j: (b, 0))
    old = pl.BlockSpec((tk * nh, LANES), lambda b, j: (b * nkv + j, 0))
    return pl.pallas_call(
        functools.partial(_attn_sample_kernel, lam_init=lam_init, past=past), grid=(B, nkv),
        in_specs=[pl.BlockSpec(lam_vecs.shape, lambda b, j: (0, 0)),
                  pl.BlockSpec((1, LANES), lambda b, j: (0, 0)),
                  new, old, old, new, new],
        out_specs=new,
        out_shape=jax.ShapeDtypeStruct((B * Ts, DW), BF16),
        scratch_shapes=[pltpu.VMEM((nh, 2 * Ts, LANES), F32)] * 3,
        compiler_params=_cp(("parallel", "arbitrary")), name="attn_sample",
    )(lam_vecs, subln_g.reshape(1, LANES), qb, cache_k, cache_v, kb, vb)


def _layer(i, x_prompt, x_sample, pe_prompt, pe_sample, cache_k, cache_v, state_rwkv, state_shift, w):
    Bp, Tp, D = x_prompt.shape
    Bs, Ts, _ = x_sample.shape
    Mp, Ms = Bp * Tp, Bs * Ts
    past = cache_k.shape[1]
    RW = w["w0"].shape[-1]
    DW = w["w_branch_diff"].shape[0]
    SW = state_shift.shape[-1]
    SWp = _round_up(SW, SEG_ALIGN)
    w_in = w["w_in"]
    lam_init = 0.8 - 0.6 * math.exp(-0.3 * i)

    x_all = jnp.concatenate([x_prompt.reshape(Mp, D), x_sample.reshape(Ms, D)], axis=0)
    h1 = _rmsnorm(x_all, w["norm1_g"])
    w_t = w_in.T
    proj_r = _mm_nt_rows(h1, w_t, 0, SWp, name="proj_rwkv")
    proj_a = _mm_nt_rows(h1, w_t, SW, w_t.shape[0] - SW, name="proj_attn")

    pad_sw = lambda a: jnp.pad(a, ((0, 0), (0, 0), (0, SWp - SW)))
    mu = jnp.pad(w["tm_mu"], (0, SWp - SW))
    lam_vecs = jnp.stack([w["lam_q1"], w["lam_k1"], w["lam_q2"], w["lam_k2"]])
    rwkv_w = (mu, w["w0"], w["a0"], w["k_k"], w["k_a"], w["r_k"].reshape(-1), w["w_decay_up"], w["a_up"], w["g_up"],
              w["lnx_g"], w["lnx_b"])

    s0_p = jnp.zeros((Bp, RW // LANES, LANES, LANES), F32)
    yr_p, s_p, last_p = _rwkv(proj_r, 0, Bp, Tp, SWp, jnp.zeros((Bp, 1, SWp), F32), s0_p, *rwkv_w)
    tab_p = _rope_tables(jnp.arange(Tp, dtype=jnp.int32))
    tab_p = tuple(jnp.tile(t, (Bp, 1)) for t in tab_p)
    qb_p, kf_p, kb_p, vf_p, vb_p = _qknorm(proj_a, 0, Mp, DW, tab_p, w["q_norm_g"], w["k_norm_g"])
    yd_p = _attn_prompt(qb_p, kb_p, vb_p, lam_vecs, w["subln_g"], Bp, Tp, lam_init)

    yr_s, s_s, last_s = _rwkv(proj_r, Mp, Bs, Ts, SWp, pad_sw(state_shift), _to_blockdiag(state_rwkv), *rwkv_w)
    tab_s = _rope_tables(past + jnp.arange(Ts, dtype=jnp.int32))
    tab_s = tuple(jnp.tile(t, (Bs, 1)) for t in tab_s)
    qb_s, kf_s, kb_s, vf_s, vb_s = _qknorm(proj_a, Mp, Ms, DW, tab_s, w["q_norm_g"], w["k_norm_g"])
    yd_s = _attn_sample(qb_s, kb_s, vb_s, cache_k.reshape(-1, LANES), cache_v.reshape(-1, LANES),
                        lam_vecs, w["subln_g"], Bs, Ts, lam_init)

    y_r = jnp.concatenate([yr_p, yr_s], axis=0)
    y_d = jnp.concatenate([yd_p, yd_s], axis=0)
    merged = _merge(y_r, y_d, w["w_branch_rwkv"], w["w_branch_diff"], proj_a, 3 * DW)
    x1 = _mm_fullk(_mm_res_kernel, merged, w["w_out"], F32, res=x_all, tn_target=256, name="out_proj")
    h2 = _rmsnorm(x1, w["norm2_g"])
    u = _mm_fullk(_mm_relu2_kernel, h2, w["w_up"], BF16, name="ffn_up")
    x2 = _mm_res_kloop(u, w["w_down"], x1, name="ffn_down")
    h3 = _rmsnorm(x2, w["ple_norm_g"])
    y_p = _ple(h3, x2, pe_prompt.reshape(Mp, -1), w["w_ple_gate"], w["w_ple_proj"], 0, Mp)
    y_s = _ple(h3, x2, pe_sample.reshape(Ms, -1), w["w_ple_gate"], w["w_ple_proj"], Mp, Ms)

    nh_d = DW // LANES
    shift_p = last_p[:, :SW].reshape(Bp, 1, SW)
    shift_s = last_s[:, :SW].reshape(Bs, 1, SW)
    return (y_p.reshape(Bp, Tp, D), y_s.reshape(Bs, Ts, D),
            kf_p.reshape(Bp, Tp, nh_d, LANES), vf_p.reshape(Bp, Tp, nh_d, LANES), _from_blockdiag(s_p), shift_p,
            kf_s.reshape(Bs, Ts, nh_d, LANES), vf_s.reshape(Bs, Ts, nh_d, LANES), _from_blockdiag(s_s), shift_s)


def kernel(x_prompt, x_sample, p_prompt, p_sample, cache_k, cache_v, state_rwkv, state_shift, norm1_g, w_in, tm_mu, w_decay_up, w0, a_up, a0, g_up, k_k, k_a, r_k, lnx_g, lnx_b, q_norm_g, k_norm_g, lam_q1, lam_k1, lam_q2, lam_k2, subln_g, w_branch_rwkv, w_branch_diff, w_out, norm2_g, w_up, w_down, ple_norm_g, w_ple_gate, w_ple_proj):
    weights = dict(norm1_g=norm1_g, w_in=w_in, tm_mu=tm_mu, w_decay_up=w_decay_up, w0=w0, a_up=a_up, a0=a0,
                   g_up=g_up, k_k=k_k, k_a=k_a, r_k=r_k, lnx_g=lnx_g, lnx_b=lnx_b, q_norm_g=q_norm_g,
                   k_norm_g=k_norm_g, lam_q1=lam_q1, lam_k1=lam_k1, lam_q2=lam_q2, lam_k2=lam_k2,
                   subln_g=subln_g, w_branch_rwkv=w_branch_rwkv, w_branch_diff=w_branch_diff, w_out=w_out,
                   norm2_g=norm2_g, w_up=w_up, w_down=w_down, ple_norm_g=ple_norm_g, w_ple_gate=w_ple_gate,
                   w_ple_proj=w_ple_proj)
    depth = w_in.shape[0]
    hp, hs = x_prompt, x_sample
    outs = [[] for _ in range(8)]
    for i in range(depth):
        res = _layer(i, hp, hs, p_prompt[i], p_sample[i], cache_k[i], cache_v[i], state_rwkv[i], state_shift[i],
                     {name: val[i] for name, val in weights.items()})
        hp, hs = res[0], res[1]
        for acc, val in zip(outs, res[2:]):
            acc.append(val)
    return (hp, hs) + tuple(jnp.stack(acc, 0) for acc in outs)
```

```python
import functools
import math

import jax
import jax.numpy as jnp
from jax import lax
from jax.experimental import pallas as pl
from jax.experimental.pallas import tpu as pltpu

F32 = jnp.float32
BF16 = jnp.bfloat16

LANES = 128
HEAD = 64
RMS_EPS = 1e-6
GN_EPS = HEAD * 1e-5
NEG_INF = -1e30
CHUNK = 64
ROT_DIM = HEAD // 4
ROPE_THETA = 500000.0
SCAN_L = 128
SCAN_PAIRS = 8
ATTN_HEADS = 4
SEG_ALIGN = 512
VMEM_LIMIT = 56 * 1024 * 1024
NT = (((1,), (1,)), ((), ()))


def _round_up(n, m):
    return (n + m - 1) // m * m


def _pick(n, target, mult):
    best = None
    for d in range(mult, min(n, target) + 1, mult):
        if n % d == 0:
            best = d
    assert best is not None, (n, target, mult)
    return best


def _cp(sem):
    return pltpu.CompilerParams(dimension_semantics=sem, vmem_limit_bytes=VMEM_LIMIT)


def _bdot(a, b):
    return jnp.dot(a.astype(BF16), b.astype(BF16), preferred_element_type=F32)


def _bdot_nt(a, b):
    return lax.dot_general(a.astype(BF16), b.astype(BF16), NT, preferred_element_type=F32)


def _head_ones():
    r = lax.broadcasted_iota(jnp.int32, (2 * LANES, LANES), 0)
    c = lax.broadcasted_iota(jnp.int32, (2 * LANES, LANES), 1)
    r = jnp.where(r >= LANES, r - LANES, r)
    return jnp.where((r >= HEAD) == (c >= HEAD), 1.0, 0.0).astype(BF16)


def _head_allsum(x, ones2):
    hi = x.astype(BF16)
    lo = (x - hi.astype(F32)).astype(BF16)
    return jnp.dot(jnp.concatenate([hi, lo], axis=1), ones2, preferred_element_type=F32)


def _rmsnorm_kernel(x_ref, g_ref, o_ref):
    x = x_ref[...]
    ms = jnp.mean(x * x, axis=-1, keepdims=True)
    o_ref[...] = (x * lax.rsqrt(ms + RMS_EPS) * g_ref[...]).astype(o_ref.dtype)


def _rmsnorm(x, g):
    M, D = x.shape
    tr = _pick(M, 256, 16)
    return pl.pallas_call(
        _rmsnorm_kernel, grid=(M // tr,),
        in_specs=[pl.BlockSpec((tr, D), lambda i: (i, 0)), pl.BlockSpec((1, D), lambda i: (0, 0))],
        out_specs=pl.BlockSpec((tr, D), lambda i: (i, 0)),
        out_shape=jax.ShapeDtypeStruct((M, D), BF16),
        compiler_params=_cp(("parallel",)), name="rmsnorm")(x, g.reshape(1, D))


def _rmsnorm_cat_kernel(xp_ref, xs_ref, g_ref, x_o, h_o, *, n_p):
    x = jnp.where(pl.program_id(0) < n_p, xp_ref[...], xs_ref[...])
    ms = jnp.mean(x * x, axis=-1, keepdims=True)
    x_o[...] = x
    h_o[...] = (x * lax.rsqrt(ms + RMS_EPS) * g_ref[...]).astype(h_o.dtype)


def _rmsnorm_cat(xp, xs, g):
    Mp, D = xp.shape
    Ms = xs.shape[0]
    tr = _pick(math.gcd(Mp, Ms), 256, 16)
    n_p = Mp // tr
    blk = pl.BlockSpec((tr, D), lambda i: (i, 0))
    return pl.pallas_call(
        functools.partial(_rmsnorm_cat_kernel, n_p=n_p), grid=((Mp + Ms) // tr,),
        in_specs=[pl.BlockSpec((tr, D), lambda i: (jnp.minimum(i, n_p - 1), 0)),
                  pl.BlockSpec((tr, D), lambda i: (jnp.maximum(i - n_p, 0), 0)),
                  pl.BlockSpec((1, D), lambda i: (0, 0))],
        out_specs=[blk, blk],
        out_shape=[jax.ShapeDtypeStruct((Mp + Ms, D), F32), jax.ShapeDtypeStruct((Mp + Ms, D), BF16)],
        compiler_params=_cp(("parallel",)), name="rmsnorm_cat")(xp, xs, g.reshape(1, D))


def _mm_relu2_kernel(a_ref, b_ref, o_ref):
    p = jnp.maximum(_bdot(a_ref[...], b_ref[...]), 0.0)
    o_ref[...] = (p * p).astype(o_ref.dtype)


def _mm_res_kernel(a_ref, b_ref, r_ref, o_ref):
    o_ref[...] = r_ref[...] + _bdot(a_ref[...], b_ref[...])


def _mm_fullk(kernel, a, b, out_dtype, res=None, tm_target=1408, tn_target=512, name="mm"):
    M, K = a.shape
    N = b.shape[1]
    tm = _pick(M, tm_target, 16)
    tn = _pick(N, tn_target, LANES)
    in_specs = [pl.BlockSpec((tm, K), lambda i, j: (i, 0)), pl.BlockSpec((K, tn), lambda i, j: (0, j))]
    args = [a, b]
    if res is not None:
        in_specs.append(pl.BlockSpec((tm, tn), lambda i, j: (i, j)))
        args.append(res)
    return pl.pallas_call(
        kernel, grid=(M // tm, N // tn), in_specs=in_specs,
        out_specs=pl.BlockSpec((tm, tn), lambda i, j: (i, j)),
        out_shape=jax.ShapeDtypeStruct((M, N), out_dtype),
        compiler_params=_cp(("parallel", "parallel")), name=name)(*args)


def _proj_kernel(a_ref, bt_ref, o_ref):
    o_ref[...] = _bdot_nt(a_ref[...], bt_ref[...])


def _proj_v_kernel(a_ref, bt_ref, vf_o, vb_o):
    v = _bdot_nt(a_ref[...], bt_ref[...])
    vf_o[...] = v
    vb_o[...] = v.astype(vb_o.dtype)


def _proj_qk_kernel(a_ref, bt_ref, c_ref, s1_ref, s2_ref, g_ref, *outs, scale):
    p = _bdot_nt(a_ref[...], bt_ref[...])
    cos, s1, s2, gain = c_ref[...], s1_ref[...], s2_ref[...], g_ref[...]
    ones2 = _head_ones()
    half = ROT_DIM // 2
    for c in range(0, p.shape[1], LANES):
        cs = slice(c, c + LANES)
        x = p[:, cs]
        ms = _head_allsum(x * x, ones2) * (1.0 / HEAD)
        y = x * lax.rsqrt(ms + RMS_EPS) * gain
        y = y * cos + pltpu.roll(y, LANES - half, axis=1) * s1 + pltpu.roll(y, half, axis=1) * s2
        if len(outs) == 2:
            outs[0][:, cs] = y
            outs[1][:, cs] = y.astype(outs[1].dtype)
        else:
            outs[0][:, cs] = (y * scale).astype(outs[0].dtype)


def _proj(kernel, a, bt, row0, n_rows, out_dtypes, tn_target, name, tables=()):
    M, K = a.shape
    tm = _pick(M, 1408, 16)
    tn = _pick(n_rows, tn_target, LANES)
    assert row0 % 8 == 0
    t_specs = [pl.BlockSpec((tm, LANES), lambda i, j: (i, 0)) for _ in tables[:-1]]
    t_specs += [pl.BlockSpec((1, LANES), lambda i, j: (0, 0)) for _ in tables[-1:]]
    return pl.pallas_call(
        kernel, grid=(M // tm, n_rows // tn),
        in_specs=[pl.BlockSpec((tm, K), lambda i, j: (i, 0)),
                  pl.BlockSpec((pl.Element(tn), pl.Element(K)),
                               lambda i, j: ((row0 // 8 + j * (tn // 8)) * 8, 0))] + t_specs,
        out_specs=[pl.BlockSpec((tm, tn), lambda i, j: (i, j))] * len(out_dtypes),
        out_shape=[jax.ShapeDtypeStruct((M, n_rows), dt) for dt in out_dtypes],
        compiler_params=_cp(("parallel", "parallel")), name=name)(a, bt, *tables)


def _mm_res_kloop_kernel(a_ref, b_ref, r_ref, o_ref):
    @pl.when(pl.program_id(2) == 0)
    def _():
        o_ref[...] = r_ref[...]

    o_ref[...] += _bdot(a_ref[...], b_ref[...])


def _mm_res_kloop(a, b, res, name):
    M, K = a.shape
    N = b.shape[1]
    tm = _pick(M, 1408, 16)
    tn = _pick(N, 1024, LANES)
    tk = _pick(K, 1024, LANES)
    return pl.pallas_call(
        _mm_res_kloop_kernel, grid=(M // tm, N // tn, K // tk),
        in_specs=[pl.BlockSpec((tm, tk), lambda i, j, k: (i, k)),
                  pl.BlockSpec((tk, tn), lambda i, j, k: (k, j)),
                  pl.BlockSpec((tm, tn), lambda i, j, k: (i, j))],
        out_specs=pl.BlockSpec((tm, tn), lambda i, j, k: (i, j)),
        out_shape=jax.ShapeDtypeStruct((M, N), F32),
        compiler_params=_cp(("parallel", "parallel", "arbitrary")), name=name)(a, b, res)


def _merge_kernel(yr_ref, yd_ref, wr_ref, wd_ref, gr_ref, gd_ref, o_ref):
    pr = _bdot(yr_ref[...], wr_ref[...])
    pd = _bdot(yd_ref[...], wd_ref[...])
    o_ref[...] = (jax.nn.sigmoid(gr_ref[...]) * pr + jax.nn.sigmoid(gd_ref[...]) * pd).astype(o_ref.dtype)


def _merge(y_r, y_d, w_r, w_d, proj, gate_off):
    M, Kr = y_r.shape
    Kd = y_d.shape[1]
    D = w_r.shape[1]
    tm = _pick(M, 1408, 16)
    tn = _pick(D, 256, LANES)
    assert gate_off % tn == 0
    g0 = gate_off // tn
    g1 = (gate_off + D) // tn
    return pl.pallas_call(
        _merge_kernel, grid=(M // tm, D // tn),
        in_specs=[pl.BlockSpec((tm, Kr), lambda i, j: (i, 0)),
                  pl.BlockSpec((tm, Kd), lambda i, j: (i, 0)),
                  pl.BlockSpec((Kr, tn), lambda i, j: (0, j)),
                  pl.BlockSpec((Kd, tn), lambda i, j: (0, j)),
                  pl.BlockSpec((tm, tn), lambda i, j: (i, g0 + j)),
                  pl.BlockSpec((tm, tn), lambda i, j: (i, g1 + j))],
        out_specs=pl.BlockSpec((tm, tn), lambda i, j: (i, j)),
        out_shape=jax.ShapeDtypeStruct((M, D), BF16),
        compiler_params=_cp(("parallel", "parallel")), name="branch_merge")(y_r, y_d, w_r, w_d, proj, proj)


def _ple_kernel(h_ref, wg_ref, pe_ref, wp_ref, x_ref, o_ref):
    gate = jax.nn.sigmoid(_bdot(h_ref[...], wg_ref[...]))
    o_ref[...] = x_ref[...] + gate * _bdot(pe_ref[...], wp_ref[...])


def _ple(h, x, pe, w_gate, w_proj, row_off, rows):
    D = x.shape[1]
    P = pe.shape[1]
    tm = _pick(rows, 1024, 16)
    tn = _pick(D, 512, LANES)
    assert row_off % tm == 0
    r0 = row_off // tm
    return pl.pallas_call(
        _ple_kernel, grid=(rows // tm, D // tn),
        in_specs=[pl.BlockSpec((tm, D), lambda i, j: (r0 + i, 0)),
                  pl.BlockSpec((D, tn), lambda i, j: (0, j)),
                  pl.BlockSpec((tm, P), lambda i, j: (i, 0)),
                  pl.BlockSpec((P, tn), lambda i, j: (0, j)),
                  pl.BlockSpec((tm, tn), lambda i, j: (r0 + i, j))],
        out_specs=pl.BlockSpec((tm, tn), lambda i, j: (i, j)),
        out_shape=jax.ShapeDtypeStruct((rows, D), F32),
        compiler_params=_cp(("parallel", "parallel")), name="ple")(h, w_gate, pe, w_proj, x)


def _head_allsum_wide(x, ones2):
    n = x.shape[1] // LANES
    rows = x.shape[0]
    s = _head_allsum(jnp.concatenate([x[:, i * LANES:(i + 1) * LANES] for i in range(n)], axis=0), ones2)
    return jnp.concatenate([s[i * rows:(i + 1) * rows] for i in range(n)], axis=1)


def _rwkv_kernel(pr_ref, pk_ref, pv_ref, pw_ref, spr_ref, spk_ref, spv_ref, spw_ref,
                 mur_ref, muk_ref, muv_ref, muw_ref, w0_ref, a0_ref, kk_ref, ka_ref, rk_ref,
                 wd_ref, wa_ref, wg_ref, lng_ref, lnb_ref, s0_ref,
                 y_ref, so_ref, last_ref, sbd_ref, prev_ref, *, PB, L, win_w, win_a, win_g):
    c = pl.program_id(2)
    Tb = pr_ref.shape[0]
    W = PB * LANES
    lane = lax.broadcasted_iota(jnp.int32, (L, LANES), 1)
    lane2 = lax.broadcasted_iota(jnp.int32, (2 * L, LANES), 1)
    row = lax.broadcasted_iota(jnp.int32, (L, L), 0)
    col = lax.broadcasted_iota(jnp.int32, (L, L), 1)
    incl = row >= col
    strict = row > col
    tri = jnp.where(incl, 1.0, 0.0).astype(BF16)
    eye = jnp.where(row == col, 1.0, 0.0)
    srow = lax.broadcasted_iota(jnp.int32, (LANES, LANES), 0)
    scol = lax.broadcasted_iota(jnp.int32, (LANES, LANES), 1)
    same_head = (srow >= HEAD) == (scol >= HEAD)
    ones2 = _head_ones()

    @pl.when(c == 0)
    def _():
        sbd_ref[...] = s0_ref[0]
        prev_ref[:, 0:W] = spr_ref[0]
        prev_ref[:, W:2 * W] = spk_ref[0]
        prev_ref[:, 2 * W:3 * W] = spv_ref[0]
        prev_ref[:, 3 * W:] = spw_ref[0]

    first_row = lax.broadcasted_iota(jnp.int32, (Tb, 1), 0) == 0

    def shifted(x_ref, mu_ref, off):
        x = x_ref[...]
        prev = prev_ref[:, off:off + x.shape[1]]
        prev_ref[:, off:off + x.shape[1]] = x[Tb - 1:Tb, :]
        sh = jnp.where(first_row, prev, pltpu.roll(x, 1, axis=0))
        return x + (sh - x) * mu_ref[...]

    r = shifted(pr_ref, mur_ref, 0)
    k = shifted(pk_ref, muk_ref, W)
    v = shifted(pv_ref, muv_ref, 2 * W)
    lo = shifted(pw_ref, muw_ref, 3 * W)
    z = w0_ref[...] + _bdot(jnp.tanh(lo[:, win_w[0]:win_w[1]]), wd_ref[...])
    lw = -math.exp(-0.5) * jax.nn.sigmoid(z)
    alr = jax.nn.sigmoid(a0_ref[...] + _bdot(lo[:, win_a[0]:win_a[1]], wa_ref[...]))
    gate = _bdot(jax.nn.sigmoid(lo[:, win_g[0]:win_g[1]]), wg_ref[...])
    kk = k * kk_ref[...]
    k2 = k * (1.0 + (alr - 1.0) * ka_ref[...])
    kkn = kk * lax.rsqrt(_head_allsum_wide(kk * kk, ones2) + 1e-12)
    bonus_gate = _head_allsum_wide(r * k2 * rk_ref[...], ones2) * v * gate
    if Tb < L:
        pad = lambda x: jnp.concatenate([x, jnp.zeros((L - Tb, W), F32)], axis=0)
        r, k2, v, lw, kkn, alr = pad(r), pad(k2), pad(v), pad(lw), pad(kkn), pad(alr)

    pairs = range(PB)
    heads = [(p, e) for p in pairs for e in range(2)]
    cols = [slice(p * LANES, (p + 1) * LANES) for p in pairs]
    bmm = lambda x, y: lax.dot_general(x.astype(BF16), y.astype(BF16), (((2,), (1,)), ((0,), (0,))),
                                       preferred_element_type=F32)
    bmm_nt = lambda x, y: lax.dot_general(x.astype(BF16), y.astype(BF16), (((2,), (2,)), ((0,), (0,))),
                                          preferred_element_type=F32)
    rows_cat = lambda xs: jnp.concatenate(xs, axis=0)

    w1 = lw.astype(BF16)
    res1 = lw - w1.astype(F32)
    w2 = res1.astype(BF16)
    w3 = (res1 - w2.astype(F32)).astype(BF16)
    lc3 = jnp.dot(tri, jnp.concatenate([w1, w2, w3], axis=1), preferred_element_type=F32)
    lc = lc3[:, :W] + lc3[:, W:2 * W] + lc3[:, 2 * W:]
    lc_last = lc[L - 1:L, :]
    g_in, g_prev, g_inv = jnp.exp(lc), jnp.exp(lc - lw), jnp.exp(-lc)
    ratio = jnp.exp(lc_last - lc)
    g_last = jnp.exp(lc_last)
    b = kkn * alr
    At, Rt, Bt, Kt = -kkn * g_prev, r * g_in, b * g_inv, k2 * g_inv
    Bh, Kh = b * ratio, k2 * ratio

    he = lane >= HEAD
    own = [lane2 < HEAD, lane2 >= HEAD]
    X = [rows_cat([At[:, cs], Rt[:, cs]]) for cs in cols]
    Y = [rows_cat([Bt[:, cs], Kt[:, cs]]) for cs in cols]
    G = bmm_nt(jnp.stack([jnp.where(own[e], X[p], 0.0) for p, e in heads]),
               jnp.stack([Y[p] for p, e in heads]))
    Nab = jnp.where(strict, G[:, :L, :L], 0.0)
    Aak = jnp.where(strict, G[:, :L, L:], 0.0)
    Arb = jnp.where(incl, G[:, L:, :L], 0.0)
    Ark = jnp.where(incl, G[:, L:, L:], 0.0)
    Tm = eye + Nab
    Np = Nab
    lev = 1
    while lev * 2 < L:
        Np = bmm(Np, Np)
        Tm = Tm + bmm(Tm, Np)
        lev *= 2
    AV = bmm(jnp.concatenate([Aak, Ark], axis=1), jnp.stack([v[:, cols[p]] for p, e in heads]))
    TW = bmm(Tm, jnp.concatenate([jnp.stack([At[:, cols[p]] for p, e in heads]), AV[:, :L]], axis=2))
    pick = lambda x, p: jnp.where(he, x[2 * p + 1], x[2 * p])
    W1 = [pick(TW[:, :, :LANES], p) for p in pairs]
    W2 = [pick(TW[:, :, LANES:], p) for p in pairs]
    Op = [pick(AV[:, L:], p) for p in pairs]
    S = sbd_ref[...]
    XS = bmm_nt(jnp.stack([rows_cat([W1[p], Rt[:, cols[p]]]) for p in pairs]), S)
    U = [XS[p, :L] + W2[p] for p in pairs]
    AU = bmm(Arb, jnp.stack([U[p] for p, e in heads]))
    O = [XS[p, L:] + Op[p] + pick(AU, p) for p in pairs]
    upd = bmm(jnp.stack([rows_cat([U[p], v[:, cols[p]]]).T for p in pairs]),
              jnp.stack([rows_cat([Bh[:, cols[p]], Kh[:, cols[p]]]) for p in pairs]))
    sbd_ref[...] = S * jnp.stack([g_last[:, cols[p]] for p in pairs]) + jnp.where(same_head, upd, 0.0)
    Oc = rows_cat(O)
    mu = _head_allsum(Oc, ones2) * (1.0 / HEAD)
    d = Oc - mu
    var = _head_allsum(d * d, ones2) * (1.0 / HEAD)
    nrm = d * lax.rsqrt(var + GN_EPS)
    for p in pairs:
        gn = nrm[p * L:p * L + Tb] * lng_ref[:, cols[p]] + lnb_ref[:, cols[p]]
        y_ref[:, cols[p]] = (gn * gate[:, cols[p]] + bonus_gate[:, cols[p]]).astype(y_ref.dtype)

    @pl.when(c == pl.num_programs(2) - 1)
    def _():
        so_ref[0] = sbd_ref[...]
        last_ref[0, 0] = prev_ref[...]


def _lora_window(start, width, w, base):
    s0 = start // LANES * LANES
    s1 = _round_up(start + width, LANES)
    wp = jnp.zeros((s1 - s0, w.shape[1]), w.dtype).at[start - s0:start - s0 + width].set(w)
    return (s0 - base, s1 - base), wp


def _rwkv(proj, row_off, B, T, SWp, shift_prev, s0_bd, mu, w0, a0, k_k, k_a, r_k, w_decay_up, a_up, g_up,
          lnx_g, lnx_b):
    RW = w0.shape[-1]
    L = SCAN_L
    Tb = min(T, L)
    assert T % Tb == 0 and row_off % Tb == 0
    nc = T // Tb
    r0 = row_off // Tb
    npair = RW // LANES
    PB = math.gcd(npair, SCAN_PAIRS)
    W = PB * LANES
    ng = RW // W
    LW = SWp - 3 * RW
    assert LW % LANES == 0 and (3 * RW) % LW == 0
    lb = 3 * RW // LW
    DL, AL, GL = w_decay_up.shape[0], a_up.shape[0], g_up.shape[0]
    win_w, wd = _lora_window(3 * RW, DL, w_decay_up, 3 * RW)
    win_a, wa = _lora_window(3 * RW + DL, AL, a_up, 3 * RW)
    win_g, wg = _lora_window(3 * RW + DL + AL, GL, g_up, 3 * RW)
    row = lambda x: x.reshape(1, -1)
    seg = lambda s: pl.BlockSpec((Tb, W), lambda b, g, c: (r0 + b * nc + c, s * ng + g))
    seg_prev = lambda s: pl.BlockSpec((1, 1, W), lambda b, g, c: (b, 0, s * ng + g))
    seg_mu = lambda s: pl.BlockSpec((1, W), lambda b, g, c: (0, s * ng + g))
    vec = pl.BlockSpec((1, W), lambda b, g, c: (0, g))
    wblk = lambda w: pl.BlockSpec((w.shape[0], W), lambda b, g, c: (0, g))
    sblk = pl.BlockSpec((1, PB, LANES, LANES), lambda b, g, c: (b, g, 0, 0))
    mu2 = row(mu)
    PW = 3 * W + LW
    y, s_out, last = pl.pallas_call(
        functools.partial(_rwkv_kernel, PB=PB, L=L, win_w=win_w, win_a=win_a, win_g=win_g),
        grid=(B, ng, nc),
        in_specs=[seg(0), seg(1), seg(2), pl.BlockSpec((Tb, LW), lambda b, g, c: (r0 + b * nc + c, lb)),
                  seg_prev(0), seg_prev(1), seg_prev(2), pl.BlockSpec((1, 1, LW), lambda b, g, c: (b, 0, lb)),
                  seg_mu(0), seg_mu(1), seg_mu(2), pl.BlockSpec((1, LW), lambda b, g, c: (0, lb)),
                  vec, vec, vec, vec, vec, wblk(wd), wblk(wa), wblk(wg), vec, vec, sblk],
        out_specs=[pl.BlockSpec((Tb, W), lambda b, g, c: (b * nc + c, g)), sblk,
                   pl.BlockSpec((1, 1, 1, PW), lambda b, g, c: (b, g, 0, 0))],
        out_shape=[jax.ShapeDtypeStruct((B * T, RW), BF16),
                   jax.ShapeDtypeStruct((B, npair, LANES, LANES), F32),
                   jax.ShapeDtypeStruct((B, ng, 1, PW), F32)],
        scratch_shapes=[pltpu.VMEM((PB, LANES, LANES), F32), pltpu.VMEM((1, PW), F32)],
        compiler_params=_cp(("parallel", "parallel", "arbitrary")), name="rwkv",
    )(proj, proj, proj, proj, shift_prev, shift_prev, shift_prev, shift_prev, mu2, mu2, mu2, mu2,
      row(w0), row(a0), row(k_k), row(k_a), row(r_k), wd, wa, wg, row(lnx_g), row(lnx_b), s0_bd)
    segs = [last[:, :, 0, s * W:(s + 1) * W].reshape(B, RW) for s in range(3)]
    shift_out = jnp.concatenate(segs + [last[:, 0, 0, 3 * W:]], axis=1)
    return y, s_out, shift_out


def _to_blockdiag(s):
    B, H = s.shape[:2]
    s = s.reshape(B, H // 2, 2, HEAD, HEAD)
    z = jnp.zeros_like(s[:, :, 0])
    top = jnp.concatenate([s[:, :, 0], z], axis=-1)
    bot = jnp.concatenate([z, s[:, :, 1]], axis=-1)
    return jnp.concatenate([top, bot], axis=-2)


def _from_blockdiag(sbd):
    B, P = sbd.shape[:2]
    return jnp.stack([sbd[:, :, :HEAD, :HEAD], sbd[:, :, HEAD:, HEAD:]], axis=2).reshape(B, 2 * P, HEAD, HEAD)


def _rope_tables(pos):
    half = ROT_DIM // 2
    inv_freq = jnp.power(jnp.float32(ROPE_THETA), -jnp.arange(0, ROT_DIM, 2, dtype=jnp.float32) / ROT_DIM)
    ang = pos.astype(jnp.float32)[:, None] * inv_freq[None, :]
    cos, sin = jnp.cos(ang), jnp.sin(ang)
    T = pos.shape[0]
    pad = jnp.zeros((T, HEAD - ROT_DIM), F32)
    z = jnp.zeros((T, half), F32)
    c = jnp.concatenate([cos, cos, pad + 1.0], axis=1)
    s1 = jnp.concatenate([-sin, z, pad], axis=1)
    s2 = jnp.concatenate([z, sin, pad], axis=1)
    tile = lambda t: jnp.concatenate([t, t], axis=1)
    return tile(c), tile(s1), tile(s2)


def _lambda(lam_ref, lam_init):
    lam = lam_ref[...]
    return (jnp.exp(jnp.sum(lam[0:1] * lam[1:2], axis=1, keepdims=True))
            - jnp.exp(jnp.sum(lam[2:3] * lam[3:4], axis=1, keepdims=True)) + lam_init)


def _stack_components(q):
    lane = lax.broadcasted_iota(jnp.int32, q.shape, 1)
    zero = jnp.zeros_like(q)
    return jnp.concatenate([jnp.where(lane < HEAD, q, zero), jnp.where(lane >= HEAD, q, zero)], axis=0)


def _softmax_step(qq, k, v, m_ref, l_ref, acc_ref, mask):
    s = lax.dot_general(qq, k, (((2,), (2,)), ((0,), (0,))), preferred_element_type=F32)
    if mask is not None:
        s = jnp.where(mask, s, NEG_INF)
    m_old = m_ref[...]
    m_new = jnp.maximum(m_old, jnp.max(s, axis=2, keepdims=True))
    alpha = jnp.exp(m_old - m_new)
    tk = s.shape[2]
    p = jnp.exp(s - jnp.concatenate([m_new] * pl.cdiv(tk, LANES), axis=2)[:, :, :tk])
    v1 = jnp.concatenate([v, jnp.ones(v.shape, v.dtype)], axis=2)
    pv = lax.dot_general(p.astype(BF16), v1, (((2,), (1,)), ((0,), (0,))), preferred_element_type=F32)
    l_ref[...] = alpha * l_ref[...] + pv[:, :, LANES:]
    acc_ref[...] = alpha * acc_ref[...] + pv[:, :, :LANES]
    m_ref[...] = m_new


def _finish_head(t, m_ref, l_ref, acc_ref, lam, g, lam_init):
    o = acc_ref[...] / l_ref[...]
    o = o[:t] - lam * o[t:]
    ms = jnp.mean(o * o, axis=-1, keepdims=True)
    return o * lax.rsqrt(ms + RMS_EPS) * g * (1.0 - lam_init)


def _attn_prompt_kernel(lam_ref, g_ref, q_ref, k_ref, v_ref, o_ref, m_ref, l_ref, acc_ref, *, lam_init):
    qi = pl.program_id(2)
    tq = q_ref.shape[0]
    nh = q_ref.shape[1] // LANES
    cols = [slice(h * LANES, (h + 1) * LANES) for h in range(nh)]
    qq = jnp.stack([_stack_components(q_ref[:, cs]) for cs in cols])
    m_ref[...] = jnp.full(m_ref.shape, NEG_INF, F32)
    l_ref[...] = jnp.zeros(l_ref.shape, F32)
    acc_ref[...] = jnp.zeros(acc_ref.shape, F32)

    def block(j, mask):
        rows = pl.ds(pl.multiple_of(j * tq, tq), tq)
        _softmax_step(qq, jnp.stack([k_ref[rows, cs] for cs in cols]), jnp.stack([v_ref[rows, cs] for cs in cols]),
                      m_ref, l_ref, acc_ref, mask)

    def body(j, carry):
        block(j, None)
        return carry

    lax.fori_loop(0, qi, body, 0)
    r = lax.broadcasted_iota(jnp.int32, (2 * tq, tq), 0)
    cidx = lax.broadcasted_iota(jnp.int32, (2 * tq, tq), 1)
    r = jnp.where(r >= tq, r - tq, r)
    block(qi, (r // CHUNK) >= (cidx // CHUNK))
    lam = _lambda(lam_ref, lam_init)
    for h, cs in enumerate(cols):
        o_ref[:, cs] = _finish_head(tq, m_ref.at[h], l_ref.at[h], acc_ref.at[h], lam, g_ref[...],
                                    lam_init).astype(o_ref.dtype)


def _attn_prompt(qb, kb, vb, lam_vecs, subln_g, B, T, lam_init):
    DW = qb.shape[1]
    nh = DW // LANES
    hg = math.gcd(nh, ATTN_HEADS)
    tq = _pick(T, 256, CHUNK)
    nq = T // tq
    return pl.pallas_call(
        functools.partial(_attn_prompt_kernel, lam_init=lam_init), grid=(B, nh // hg, nq),
        in_specs=[pl.BlockSpec(lam_vecs.shape, lambda b, h, i: (0, 0)),
                  pl.BlockSpec((1, LANES), lambda b, h, i: (0, 0)),
                  pl.BlockSpec((tq, hg * LANES), lambda b, h, i: (b * nq + i, h)),
                  pl.BlockSpec((T, hg * LANES), lambda b, h, i: (b, h)),
                  pl.BlockSpec((T, hg * LANES), lambda b, h, i: (b, h))],
        out_specs=pl.BlockSpec((tq, hg * LANES), lambda b, h, i: (b * nq + i, h)),
        out_shape=jax.ShapeDtypeStruct((B * T, DW), BF16),
        scratch_shapes=[pltpu.VMEM((hg, 2 * tq, LANES), F32)] * 3,
        compiler_params=_cp(("parallel", "parallel", "arbitrary")), name="attn_prompt",
    )(lam_vecs, subln_g.reshape(1, LANES), qb, kb, vb)


def _attn_sample_kernel(lam_ref, g_ref, q_ref, kc_ref, vc_ref, kn_ref, vn_ref, o_ref, m_ref, l_ref, acc_ref,
                        *, lam_init, past):
    j = pl.program_id(1)
    ts = q_ref.shape[0]
    nh = q_ref.shape[1] // LANES

    @pl.when(j == 0)
    def _():
        m_ref[...] = jnp.full(m_ref.shape, NEG_INF, F32)
        l_ref[...] = jnp.zeros(l_ref.shape, F32)
        acc_ref[...] = jnp.zeros(acc_ref.shape, F32)

    cols = [slice(h * LANES, (h + 1) * LANES) for h in range(nh)]
    tk = kc_ref.shape[0] // nh

    def heads(head_k, head_v, mask):
        qq = jnp.stack([_stack_components(q_ref[:, cs]) for cs in cols])
        _softmax_step(qq, jnp.stack([head_k(h).astype(BF16) for h in range(nh)]),
                      jnp.stack([head_v(h).astype(BF16) for h in range(nh)]), m_ref, l_ref, acc_ref, mask)

    heads(lambda h: kc_ref[pl.ds(h, tk, stride=nh), :], lambda h: vc_ref[pl.ds(h, tk, stride=nh), :], None)

    @pl.when(j == pl.num_programs(1) - 1)
    def _():
        r = lax.broadcasted_iota(jnp.int32, (2 * ts, ts), 0)
        cidx = lax.broadcasted_iota(jnp.int32, (2 * ts, ts), 1)
        r = jnp.where(r >= ts, r - ts, r)
        heads(lambda h: kn_ref[:, cols[h]], lambda h: vn_ref[:, cols[h]],
              ((past + r) // CHUNK) >= ((past + cidx) // CHUNK))
        lam = _lambda(lam_ref, lam_init)
        for h in range(nh):
            cs = slice(h * LANES, (h + 1) * LANES)
            o_ref[:, cs] = _finish_head(ts, m_ref.at[h], l_ref.at[h], acc_ref.at[h], lam, g_ref[...],
                                        lam_init).astype(o_ref.dtype)


def _attn_sample(qb, kb, vb, row_off, cache_k, cache_v, lam_vecs, subln_g, B, Ts, lam_init):
    DW = qb.shape[1]
    nh = DW // LANES
    past = cache_k.shape[0] // (B * nh)
    tk = _pick(past, 512, 16)
    nkv = past // tk
    assert row_off % Ts == 0
    r0 = row_off // Ts
    new = pl.BlockSpec((Ts, DW), lambda b, j: (r0 + b, 0))
    old = pl.BlockSpec((tk * nh, LANES), lambda b, j: (b * nkv + j, 0))
    return pl.pallas_call(
        functools.partial(_attn_sample_kernel, lam_init=lam_init, past=past), grid=(B, nkv),
        in_specs=[pl.BlockSpec(lam_vecs.shape, lambda b, j: (0, 0)),
                  pl.BlockSpec((1, LANES), lambda b, j: (0, 0)),
                  new, old, old, new, new],
        out_specs=pl.BlockSpec((Ts, DW), lambda b, j: (b, 0)),
        out_shape=jax.ShapeDtypeStruct((B * Ts, DW), BF16),
        scratch_shapes=[pltpu.VMEM((nh, 2 * Ts, LANES), F32)] * 3,
        compiler_params=_cp(("parallel", "arbitrary")), name="attn_sample",
    )(lam_vecs, subln_g.reshape(1, LANES), qb, cache_k, cache_v, kb, vb)


def _layer(i, x_prompt, x_sample, pe_prompt, pe_sample, cache_k, cache_v, state_rwkv, state_shift, w):
    Bp, Tp, D = x_prompt.shape
    Bs, Ts, _ = x_sample.shape
    Mp, Ms = Bp * Tp, Bs * Ts
    past = cache_k.shape[1]
    RW = w["w0"].shape[-1]
    DW = w["w_branch_diff"].shape[0]
    SW = state_shift.shape[-1]
    SWp = _round_up(SW, SEG_ALIGN)
    w_in = w["w_in"]
    lam_init = 0.8 - 0.6 * math.exp(-0.3 * i)

    x_all, h1 = _rmsnorm_cat(x_prompt.reshape(Mp, D), x_sample.reshape(Ms, D), w["norm1_g"])
    w_t = w_in.T
    (proj_r,) = _proj(_proj_kernel, h1, w_t, 0, SWp, [F32], 512, "proj_rwkv")
    pos = jnp.concatenate([jnp.tile(jnp.arange(Tp, dtype=jnp.int32), Bp),
                           jnp.tile(past + jnp.arange(Ts, dtype=jnp.int32), Bs)])
    gain2 = lambda g: jnp.concatenate([g, g]).reshape(1, LANES)
    rope = _rope_tables(pos)
    (qb,) = _proj(functools.partial(_proj_qk_kernel, scale=HEAD ** -0.5), h1, w_t, SW, DW, [BF16], 512,
                  "proj_q", tables=rope + (gain2(w["q_norm_g"]),))
    kf, kb = _proj(functools.partial(_proj_qk_kernel, scale=None), h1, w_t, SW + DW, DW, [F32, BF16], 256,
                   "proj_k", tables=rope + (gain2(w["k_norm_g"]),))
    vf, vb = _proj(_proj_v_kernel, h1, w_t, SW + 2 * DW, DW, [F32, BF16], 256, "proj_v")
    (gates,) = _proj(_proj_kernel, h1, w_t, SW + 3 * DW, 2 * D, [F32], 512, "proj_gates")

    pad_sw = lambda a: jnp.pad(a, ((0, 0), (0, 0), (0, SWp - SW)))
    mu = jnp.pad(w["tm_mu"], (0, SWp - SW))
    lam_vecs = jnp.stack([w["lam_q1"], w["lam_k1"], w["lam_q2"], w["lam_k2"]])
    rwkv_w = (mu, w["w0"], w["a0"], w["k_k"], w["k_a"], w["r_k"].reshape(-1), w["w_decay_up"], w["a_up"], w["g_up"],
              w["lnx_g"], w["lnx_b"])

    s0_p = jnp.zeros((Bp, RW // LANES, LANES, LANES), F32)
    yr_p, s_p, last_p = _rwkv(proj_r, 0, Bp, Tp, SWp, jnp.zeros((Bp, 1, SWp), F32), s0_p, *rwkv_w)
    yd_p = _attn_prompt(qb, kb, vb, lam_vecs, w["subln_g"], Bp, Tp, lam_init)

    yr_s, s_s, last_s = _rwkv(proj_r, Mp, Bs, Ts, SWp, pad_sw(state_shift), _to_blockdiag(state_rwkv), *rwkv_w)
    yd_s = _attn_sample(qb, kb, vb, Mp, cache_k.reshape(-1, LANES), cache_v.reshape(-1, LANES),
                        lam_vecs, w["subln_g"], Bs, Ts, lam_init)

    y_r = jnp.concatenate([yr_p, yr_s], axis=0)
    y_d = jnp.concatenate([yd_p, yd_s], axis=0)
    merged = _merge(y_r, y_d, w["w_branch_rwkv"], w["w_branch_diff"], gates, 0)
    x1 = _mm_fullk(_mm_res_kernel, merged, w["w_out"], F32, res=x_all, tn_target=256, name="out_proj")
    h2 = _rmsnorm(x1, w["norm2_g"])
    u = _mm_fullk(_mm_relu2_kernel, h2, w["w_up"], BF16, name="ffn_up")
    x2 = _mm_res_kloop(u, w["w_down"], x1, name="ffn_down")
    h3 = _rmsnorm(x2, w["ple_norm_g"])
    y_p = _ple(h3, x2, pe_prompt.reshape(Mp, -1), w["w_ple_gate"], w["w_ple_proj"], 0, Mp)
    y_s = _ple(h3, x2, pe_sample.reshape(Ms, -1), w["w_ple_gate"], w["w_ple_proj"], Mp, Ms)

    nh_d = DW // LANES
    shift_p = last_p[:, :SW].reshape(Bp, 1, SW)
    shift_s = last_s[:, :SW].reshape(Bs, 1, SW)
    return (y_p.reshape(Bp, Tp, D), y_s.reshape(Bs, Ts, D),
            kf[:Mp].reshape(Bp, Tp, nh_d, LANES), vf[:Mp].reshape(Bp, Tp, nh_d, LANES), _from_blockdiag(s_p), shift_p,
            kf[Mp:].reshape(Bs, Ts, nh_d, LANES), vf[Mp:].reshape(Bs, Ts, nh_d, LANES), _from_blockdiag(s_s), shift_s)


def kernel(x_prompt, x_sample, p_prompt, p_sample, cache_k, cache_v, state_rwkv, state_shift, norm1_g, w_in, tm_mu, w_decay_up, w0, a_up, a0, g_up, k_k, k_a, r_k, lnx_g, lnx_b, q_norm_g, k_norm_g, lam_q1, lam_k1, lam_q2, lam_k2, subln_g, w_branch_rwkv, w_branch_diff, w_out, norm2_g, w_up, w_down, ple_norm_g, w_ple_gate, w_ple_proj):
    weights = dict(norm1_g=norm1_g, w_in=w_in, tm_mu=tm_mu, w_decay_up=w_decay_up, w0=w0, a_up=a_up, a0=a0,
                   g_up=g_up, k_k=k_k, k_a=k_a, r_k=r_k, lnx_g=lnx_g, lnx_b=lnx_b, q_norm_g=q_norm_g,
                   k_norm_g=k_norm_g, lam_q1=lam_q1, lam_k1=lam_k1, lam_q2=lam_q2, lam_k2=lam_k2,
                   subln_g=subln_g, w_branch_rwkv=w_branch_rwkv, w_branch_diff=w_branch_diff, w_out=w_out,
                   norm2_g=norm2_g, w_up=w_up, w_down=w_down, ple_norm_g=ple_norm_g, w_ple_gate=w_ple_gate,
                   w_ple_proj=w_ple_proj)
    depth = w_in.shape[0]
    hp, hs = x_prompt, x_sample
    outs = [[] for _ in range(8)]
    for i in range(depth):
        res = _layer(i, hp, hs, p_prompt[i], p_sample[i], cache_k[i], cache_v[i], state_rwkv[i], state_shift[i],
                     {name: val[i] for name, val in weights.items()})
        hp, hs = res[0], res[1]
        for acc, val in zip(outs, res[2:]):
            acc.append(val)
    return (hp, hs) + tuple(jnp.stack(acc, 0) for acc in outs)
```

```python
import functools
import math

import jax
import jax.numpy as jnp
from jax import lax
from jax.experimental import pallas as pl
from jax.experimental.pallas import tpu as pltpu

F32 = jnp.float32
BF16 = jnp.bfloat16

LANES = 128
HEAD = 64
RMS_EPS = 1e-6
GN_EPS = HEAD * 1e-5
NEG_INF = -1e30
CHUNK = 64
ROT_DIM = HEAD // 4
ROPE_THETA = 500000.0
SCAN_L = 128
SCAN_PAIRS = 8
ATTN_HEADS = 4
SEG_ALIGN = 512
VMEM_LIMIT = 56 * 1024 * 1024
NT = (((1,), (1,)), ((), ()))


def _round_up(n, m):
    return (n + m - 1) // m * m


def _pick(n, target, mult):
    best = None
    for d in range(mult, min(n, target) + 1, mult):
        if n % d == 0:
            best = d
    assert best is not None, (n, target, mult)
    return best


def _cp(sem):
    return pltpu.CompilerParams(dimension_semantics=sem, vmem_limit_bytes=VMEM_LIMIT)


def _bdot(a, b):
    return jnp.dot(a.astype(BF16), b.astype(BF16), preferred_element_type=F32)


def _bdot_nt(a, b):
    return lax.dot_general(a.astype(BF16), b.astype(BF16), NT, preferred_element_type=F32)


def _head_ones():
    r = lax.broadcasted_iota(jnp.int32, (2 * LANES, LANES), 0)
    c = lax.broadcasted_iota(jnp.int32, (2 * LANES, LANES), 1)
    r = jnp.where(r >= LANES, r - LANES, r)
    return jnp.where((r >= HEAD) == (c >= HEAD), 1.0, 0.0).astype(BF16)


def _head_allsum(x, ones2):
    hi = x.astype(BF16)
    lo = (x - hi.astype(F32)).astype(BF16)
    return jnp.dot(jnp.concatenate([hi, lo], axis=1), ones2, preferred_element_type=F32)


def _rmsnorm_kernel(x_ref, g_ref, o_ref):
    x = x_ref[...]
    ms = jnp.mean(x * x, axis=-1, keepdims=True)
    o_ref[...] = (x * lax.rsqrt(ms + RMS_EPS) * g_ref[...]).astype(o_ref.dtype)


def _rmsnorm(x, g):
    M, D = x.shape
    tr = _pick(M, 256, 16)
    return pl.pallas_call(
        _rmsnorm_kernel, grid=(M // tr,),
        in_specs=[pl.BlockSpec((tr, D), lambda i: (i, 0)), pl.BlockSpec((1, D), lambda i: (0, 0))],
        out_specs=pl.BlockSpec((tr, D), lambda i: (i, 0)),
        out_shape=jax.ShapeDtypeStruct((M, D), BF16),
        compiler_params=_cp(("parallel",)), name="rmsnorm")(x, g.reshape(1, D))


def _rmsnorm_cat_kernel(xp_ref, xs_ref, g_ref, x_o, h_o, *, n_p):
    x = jnp.where(pl.program_id(0) < n_p, xp_ref[...], xs_ref[...])
    ms = jnp.mean(x * x, axis=-1, keepdims=True)
    x_o[...] = x
    h_o[...] = (x * lax.rsqrt(ms + RMS_EPS) * g_ref[...]).astype(h_o.dtype)


def _rmsnorm_cat(xp, xs, g):
    Mp, D = xp.shape
    Ms = xs.shape[0]
    tr = _pick(math.gcd(Mp, Ms), 256, 16)
    n_p = Mp // tr
    blk = pl.BlockSpec((tr, D), lambda i: (i, 0))
    return pl.pallas_call(
        functools.partial(_rmsnorm_cat_kernel, n_p=n_p), grid=((Mp + Ms) // tr,),
        in_specs=[pl.BlockSpec((tr, D), lambda i: (jnp.minimum(i, n_p - 1), 0)),
                  pl.BlockSpec((tr, D), lambda i: (jnp.maximum(i - n_p, 0), 0)),
                  pl.BlockSpec((1, D), lambda i: (0, 0))],
        out_specs=[blk, blk],
        out_shape=[jax.ShapeDtypeStruct((Mp + Ms, D), F32), jax.ShapeDtypeStruct((Mp + Ms, D), BF16)],
        compiler_params=_cp(("parallel",)), name="rmsnorm_cat")(xp, xs, g.reshape(1, D))


def _mm_relu2_kernel(a_ref, b_ref, o_ref):
    p = jnp.maximum(_bdot(a_ref[...], b_ref[...]), 0.0)
    o_ref[...] = (p * p).astype(o_ref.dtype)


def _mm_res_kernel(a_ref, b_ref, r_ref, o_ref):
    o_ref[...] = r_ref[...] + _bdot(a_ref[...], b_ref[...])


def _mm_fullk(kernel, a, b, out_dtype, res=None, tm_target=1408, tn_target=512, name="mm"):
    M, K = a.shape
    N = b.shape[1]
    tm = _pick(M, tm_target, 16)
    tn = _pick(N, tn_target, LANES)
    in_specs = [pl.BlockSpec((tm, K), lambda i, j: (i, 0)), pl.BlockSpec((K, tn), lambda i, j: (0, j))]
    args = [a, b]
    if res is not None:
        in_specs.append(pl.BlockSpec((tm, tn), lambda i, j: (i, j)))
        args.append(res)
    return pl.pallas_call(
        kernel, grid=(M // tm, N // tn), in_specs=in_specs,
        out_specs=pl.BlockSpec((tm, tn), lambda i, j: (i, j)),
        out_shape=jax.ShapeDtypeStruct((M, N), out_dtype),
        compiler_params=_cp(("parallel", "parallel")), name=name)(*args)


def _mm_nt_kernel(a_ref, bt_ref, o_ref):
    o_ref[...] = _bdot_nt(a_ref[...], bt_ref[...])


def _mm_nt_rows(a, bt, row0, n_rows, name):
    M, K = a.shape
    tm = _pick(M, 1408, 16)
    tn = _pick(n_rows, 512, LANES)
    assert row0 % 8 == 0
    return pl.pallas_call(
        _mm_nt_kernel, grid=(M // tm, n_rows // tn),
        in_specs=[pl.BlockSpec((tm, K), lambda i, j: (i, 0)),
                  pl.BlockSpec((pl.Element(tn), pl.Element(K)),
                               lambda i, j: ((row0 // 8 + j * (tn // 8)) * 8, 0))],
        out_specs=pl.BlockSpec((tm, tn), lambda i, j: (i, j)),
        out_shape=jax.ShapeDtypeStruct((M, n_rows), F32),
        compiler_params=_cp(("parallel", "parallel")), name=name)(a, bt)


def _mm_res_kloop_kernel(a_ref, b_ref, r_ref, o_ref):
    @pl.when(pl.program_id(2) == 0)
    def _():
        o_ref[...] = r_ref[...]

    o_ref[...] += _bdot(a_ref[...], b_ref[...])


def _mm_res_kloop(a, b, res, name):
    M, K = a.shape
    N = b.shape[1]
    tm = _pick(M, 1408, 16)
    tn = _pick(N, 1024, LANES)
    tk = _pick(K, 1024, LANES)
    return pl.pallas_call(
        _mm_res_kloop_kernel, grid=(M // tm, N // tn, K // tk),
        in_specs=[pl.BlockSpec((tm, tk), lambda i, j, k: (i, k)),
                  pl.BlockSpec((tk, tn), lambda i, j, k: (k, j)),
                  pl.BlockSpec((tm, tn), lambda i, j, k: (i, j))],
        out_specs=pl.BlockSpec((tm, tn), lambda i, j, k: (i, j)),
        out_shape=jax.ShapeDtypeStruct((M, N), F32),
        compiler_params=_cp(("parallel", "parallel", "arbitrary")), name=name)(a, b, res)


def _merge_kernel(yr_ref, yd_ref, wr_ref, wd_ref, gr_ref, gd_ref, o_ref):
    pr = _bdot(yr_ref[...], wr_ref[...])
    pd = _bdot(yd_ref[...], wd_ref[...])
    o_ref[...] = (jax.nn.sigmoid(gr_ref[...]) * pr + jax.nn.sigmoid(gd_ref[...]) * pd).astype(o_ref.dtype)


def _merge(y_r, y_d, w_r, w_d, proj, gate_off):
    M, Kr = y_r.shape
    Kd = y_d.shape[1]
    D = w_r.shape[1]
    tm = _pick(M, 1408, 16)
    tn = _pick(D, 256, LANES)
    assert gate_off % tn == 0
    g0 = gate_off // tn
    g1 = (gate_off + D) // tn
    return pl.pallas_call(
        _merge_kernel, grid=(M // tm, D // tn),
        in_specs=[pl.BlockSpec((tm, Kr), lambda i, j: (i, 0)),
                  pl.BlockSpec((tm, Kd), lambda i, j: (i, 0)),
                  pl.BlockSpec((Kr, tn), lambda i, j: (0, j)),
                  pl.BlockSpec((Kd, tn), lambda i, j: (0, j)),
                  pl.BlockSpec((tm, tn), lambda i, j: (i, g0 + j)),
                  pl.BlockSpec((tm, tn), lambda i, j: (i, g1 + j))],
        out_specs=pl.BlockSpec((tm, tn), lambda i, j: (i, j)),
        out_shape=jax.ShapeDtypeStruct((M, D), BF16),
        compiler_params=_cp(("parallel", "parallel")), name="branch_merge")(y_r, y_d, w_r, w_d, proj, proj)


def _ple_kernel(h_ref, wg_ref, pe_ref, wp_ref, x_ref, o_ref):
    gate = jax.nn.sigmoid(_bdot(h_ref[...], wg_ref[...]))
    o_ref[...] = x_ref[...] + gate * _bdot(pe_ref[...], wp_ref[...])


def _ple(h, x, pe, w_gate, w_proj, row_off, rows):
    D = x.shape[1]
    P = pe.shape[1]
    tm = _pick(rows, 1024, 16)
    tn = _pick(D, 512, LANES)
    assert row_off % tm == 0
    r0 = row_off // tm
    return pl.pallas_call(
        _ple_kernel, grid=(rows // tm, D // tn),
        in_specs=[pl.BlockSpec((tm, D), lambda i, j: (r0 + i, 0)),
                  pl.BlockSpec((D, tn), lambda i, j: (0, j)),
                  pl.BlockSpec((tm, P), lambda i, j: (i, 0)),
                  pl.BlockSpec((P, tn), lambda i, j: (0, j)),
                  pl.BlockSpec((tm, tn), lambda i, j: (r0 + i, j))],
        out_specs=pl.BlockSpec((tm, tn), lambda i, j: (i, j)),
        out_shape=jax.ShapeDtypeStruct((rows, D), F32),
        compiler_params=_cp(("parallel", "parallel")), name="ple")(h, w_gate, pe, w_proj, x)


def _head_allsum_wide(x, ones2):
    n = x.shape[1] // LANES
    rows = x.shape[0]
    s = _head_allsum(jnp.concatenate([x[:, i * LANES:(i + 1) * LANES] for i in range(n)], axis=0), ones2)
    return jnp.concatenate([s[i * rows:(i + 1) * rows] for i in range(n)], axis=1)


def _rwkv_kernel(pr_ref, pk_ref, pv_ref, pw_ref, spr_ref, spk_ref, spv_ref, spw_ref,
                 mur_ref, muk_ref, muv_ref, muw_ref, w0_ref, a0_ref, kk_ref, ka_ref, rk_ref,
                 wd_ref, wa_ref, wg_ref, lng_ref, lnb_ref, s0_ref,
                 y_ref, so_ref, last_ref, sbd_ref, prev_ref, *, PB, L, win_w, win_a, win_g):
    c = pl.program_id(2)
    Tb = pr_ref.shape[0]
    W = PB * LANES
    lane = lax.broadcasted_iota(jnp.int32, (L, LANES), 1)
    lane2 = lax.broadcasted_iota(jnp.int32, (2 * L, LANES), 1)
    row = lax.broadcasted_iota(jnp.int32, (L, L), 0)
    col = lax.broadcasted_iota(jnp.int32, (L, L), 1)
    incl = row >= col
    strict = row > col
    tri = jnp.where(incl, 1.0, 0.0).astype(BF16)
    eye = jnp.where(row == col, 1.0, 0.0)
    srow = lax.broadcasted_iota(jnp.int32, (LANES, LANES), 0)
    scol = lax.broadcasted_iota(jnp.int32, (LANES, LANES), 1)
    same_head = (srow >= HEAD) == (scol >= HEAD)
    ones2 = _head_ones()

    @pl.when(c == 0)
    def _():
        sbd_ref[...] = s0_ref[0]
        prev_ref[:, 0:W] = spr_ref[0]
        prev_ref[:, W:2 * W] = spk_ref[0]
        prev_ref[:, 2 * W:3 * W] = spv_ref[0]
        prev_ref[:, 3 * W:] = spw_ref[0]

    first_row = lax.broadcasted_iota(jnp.int32, (Tb, 1), 0) == 0

    def shifted(x_ref, mu_ref, off):
        x = x_ref[...]
        prev = prev_ref[:, off:off + x.shape[1]]
        prev_ref[:, off:off + x.shape[1]] = x[Tb - 1:Tb, :]
        sh = jnp.where(first_row, prev, pltpu.roll(x, 1, axis=0))
        return x + (sh - x) * mu_ref[...]

    r = shifted(pr_ref, mur_ref, 0)
    k = shifted(pk_ref, muk_ref, W)
    v = shifted(pv_ref, muv_ref, 2 * W)
    lo = shifted(pw_ref, muw_ref, 3 * W)
    z = w0_ref[...] + _bdot(jnp.tanh(lo[:, win_w[0]:win_w[1]]), wd_ref[...])
    lw = -math.exp(-0.5) * jax.nn.sigmoid(z)
    alr = jax.nn.sigmoid(a0_ref[...] + _bdot(lo[:, win_a[0]:win_a[1]], wa_ref[...]))
    gate = _bdot(jax.nn.sigmoid(lo[:, win_g[0]:win_g[1]]), wg_ref[...])
    kk = k * kk_ref[...]
    k2 = k * (1.0 + (alr - 1.0) * ka_ref[...])
    kkn = kk * lax.rsqrt(_head_allsum_wide(kk * kk, ones2) + 1e-12)
    bonus_gate = _head_allsum_wide(r * k2 * rk_ref[...], ones2) * v * gate
    if Tb < L:
        pad = lambda x: jnp.concatenate([x, jnp.zeros((L - Tb, W), F32)], axis=0)
        r, k2, v, lw, kkn, alr = pad(r), pad(k2), pad(v), pad(lw), pad(kkn), pad(alr)

    pairs = range(PB)
    heads = [(p, e) for p in pairs for e in range(2)]
    cols = [slice(p * LANES, (p + 1) * LANES) for p in pairs]
    bmm = lambda x, y: lax.dot_general(x.astype(BF16), y.astype(BF16), (((2,), (1,)), ((0,), (0,))),
                                       preferred_element_type=F32)
    bmm_nt = lambda x, y: lax.dot_general(x.astype(BF16), y.astype(BF16), (((2,), (2,)), ((0,), (0,))),
                                          preferred_element_type=F32)
    rows_cat = lambda xs: jnp.concatenate(xs, axis=0)

    w1 = lw.astype(BF16)
    res1 = lw - w1.astype(F32)
    w2 = res1.astype(BF16)
    w3 = (res1 - w2.astype(F32)).astype(BF16)
    lc3 = jnp.dot(tri, jnp.concatenate([w1, w2, w3], axis=1), preferred_element_type=F32)
    lc = lc3[:, :W] + lc3[:, W:2 * W] + lc3[:, 2 * W:]
    lc_last = lc[L - 1:L, :]
    g_in, g_prev, g_inv = jnp.exp(lc), jnp.exp(lc - lw), jnp.exp(-lc)
    ratio = jnp.exp(lc_last - lc)
    g_last = jnp.exp(lc_last)
    b = kkn * alr
    At, Rt, Bt, Kt = -kkn * g_prev, r * g_in, b * g_inv, k2 * g_inv
    Bh, Kh = b * ratio, k2 * ratio

    he = lane >= HEAD
    own = [lane2 < HEAD, lane2 >= HEAD]
    X = [rows_cat([At[:, cs], Rt[:, cs]]) for cs in cols]
    Y = [rows_cat([Bt[:, cs], Kt[:, cs]]) for cs in cols]
    G = bmm_nt(jnp.stack([jnp.where(own[e], X[p], 0.0) for p, e in heads]),
               jnp.stack([Y[p] for p, e in heads]))
    Nab = jnp.where(strict, G[:, :L, :L], 0.0)
    Aak = jnp.where(strict, G[:, :L, L:], 0.0)
    Arb = jnp.where(incl, G[:, L:, :L], 0.0)
    Ark = jnp.where(incl, G[:, L:, L:], 0.0)
    Tm = eye + Nab
    Np = Nab
    lev = 1
    while lev * 2 < L:
        Np = bmm(Np, Np)
        Tm = Tm + bmm(Tm, Np)
        lev *= 2
    AV = bmm(jnp.concatenate([Aak, Ark], axis=1), jnp.stack([v[:, cols[p]] for p, e in heads]))
    TW = bmm(Tm, jnp.concatenate([jnp.stack([At[:, cols[p]] for p, e in heads]), AV[:, :L]], axis=2))
    pick = lambda x, p: jnp.where(he, x[2 * p + 1], x[2 * p])
    W1 = [pick(TW[:, :, :LANES], p) for p in pairs]
    W2 = [pick(TW[:, :, LANES:], p) for p in pairs]
    Op = [pick(AV[:, L:], p) for p in pairs]
    S = sbd_ref[...]
    XS = bmm_nt(jnp.stack([rows_cat([W1[p], Rt[:, cols[p]]]) for p in pairs]), S)
    U = [XS[p, :L] + W2[p] for p in pairs]
    AU = bmm(Arb, jnp.stack([U[p] for p, e in heads]))
    O = [XS[p, L:] + Op[p] + pick(AU, p) for p in pairs]
    upd = bmm(jnp.stack([rows_cat([U[p], v[:, cols[p]]]).T for p in pairs]),
              jnp.stack([rows_cat([Bh[:, cols[p]], Kh[:, cols[p]]]) for p in pairs]))
    sbd_ref[...] = S * jnp.stack([g_last[:, cols[p]] for p in pairs]) + jnp.where(same_head, upd, 0.0)
    Oc = rows_cat(O)
    mu = _head_allsum(Oc, ones2) * (1.0 / HEAD)
    d = Oc - mu
    var = _head_allsum(d * d, ones2) * (1.0 / HEAD)
    nrm = d * lax.rsqrt(var + GN_EPS)
    for p in pairs:
        gn = nrm[p * L:p * L + Tb] * lng_ref[:, cols[p]] + lnb_ref[:, cols[p]]
        y_ref[:, cols[p]] = (gn * gate[:, cols[p]] + bonus_gate[:, cols[p]]).astype(y_ref.dtype)

    @pl.when(c == pl.num_programs(2) - 1)
    def _():
        so_ref[0] = sbd_ref[...]
        last_ref[0, 0] = prev_ref[...]


def _lora_window(start, width, w, base):
    s0 = start // LANES * LANES
    s1 = _round_up(start + width, LANES)
    wp = jnp.zeros((s1 - s0, w.shape[1]), w.dtype).at[start - s0:start - s0 + width].set(w)
    return (s0 - base, s1 - base), wp


def _rwkv(proj, row_off, B, T, SWp, shift_prev, s0_bd, mu, w0, a0, k_k, k_a, r_k, w_decay_up, a_up, g_up,
          lnx_g, lnx_b):
    RW = w0.shape[-1]
    L = SCAN_L
    Tb = min(T, L)
    assert T % Tb == 0 and row_off % Tb == 0
    nc = T // Tb
    r0 = row_off // Tb
    npair = RW // LANES
    PB = math.gcd(npair, SCAN_PAIRS)
    W = PB * LANES
    ng = RW // W
    LW = SWp - 3 * RW
    assert LW % LANES == 0 and (3 * RW) % LW == 0
    lb = 3 * RW // LW
    DL, AL, GL = w_decay_up.shape[0], a_up.shape[0], g_up.shape[0]
    win_w, wd = _lora_window(3 * RW, DL, w_decay_up, 3 * RW)
    win_a, wa = _lora_window(3 * RW + DL, AL, a_up, 3 * RW)
    win_g, wg = _lora_window(3 * RW + DL + AL, GL, g_up, 3 * RW)
    row = lambda x: x.reshape(1, -1)
    seg = lambda s: pl.BlockSpec((Tb, W), lambda b, g, c: (r0 + b * nc + c, s * ng + g))
    seg_prev = lambda s: pl.BlockSpec((1, 1, W), lambda b, g, c: (b, 0, s * ng + g))
    seg_mu = lambda s: pl.BlockSpec((1, W), lambda b, g, c: (0, s * ng + g))
    vec = pl.BlockSpec((1, W), lambda b, g, c: (0, g))
    wblk = lambda w: pl.BlockSpec((w.shape[0], W), lambda b, g, c: (0, g))
    sblk = pl.BlockSpec((1, PB, LANES, LANES), lambda b, g, c: (b, g, 0, 0))
    mu2 = row(mu)
    PW = 3 * W + LW
    y, s_out, last = pl.pallas_call(
        functools.partial(_rwkv_kernel, PB=PB, L=L, win_w=win_w, win_a=win_a, win_g=win_g),
        grid=(B, ng, nc),
        in_specs=[seg(0), seg(1), seg(2), pl.BlockSpec((Tb, LW), lambda b, g, c: (r0 + b * nc + c, lb)),
                  seg_prev(0), seg_prev(1), seg_prev(2), pl.BlockSpec((1, 1, LW), lambda b, g, c: (b, 0, lb)),
                  seg_mu(0), seg_mu(1), seg_mu(2), pl.BlockSpec((1, LW), lambda b, g, c: (0, lb)),
                  vec, vec, vec, vec, vec, wblk(wd), wblk(wa), wblk(wg), vec, vec, sblk],
        out_specs=[pl.BlockSpec((Tb, W), lambda b, g, c: (b * nc + c, g)), sblk,
                   pl.BlockSpec((1, 1, 1, PW), lambda b, g, c: (b, g, 0, 0))],
        out_shape=[jax.ShapeDtypeStruct((B * T, RW), BF16),
                   jax.ShapeDtypeStruct((B, npair, LANES, LANES), F32),
                   jax.ShapeDtypeStruct((B, ng, 1, PW), F32)],
        scratch_shapes=[pltpu.VMEM((PB, LANES, LANES), F32), pltpu.VMEM((1, PW), F32)],
        compiler_params=_cp(("parallel", "parallel", "arbitrary")), name="rwkv",
    )(proj, proj, proj, proj, shift_prev, shift_prev, shift_prev, shift_prev, mu2, mu2, mu2, mu2,
      row(w0), row(a0), row(k_k), row(k_a), row(r_k), wd, wa, wg, row(lnx_g), row(lnx_b), s0_bd)
    segs = [last[:, :, 0, s * W:(s + 1) * W].reshape(B, RW) for s in range(3)]
    shift_out = jnp.concatenate(segs + [last[:, 0, 0, 3 * W:]], axis=1)
    return y, s_out, shift_out


def _to_blockdiag(s):
    B, H = s.shape[:2]
    s = s.reshape(B, H // 2, 2, HEAD, HEAD)
    z = jnp.zeros_like(s[:, :, 0])
    top = jnp.concatenate([s[:, :, 0], z], axis=-1)
    bot = jnp.concatenate([z, s[:, :, 1]], axis=-1)
    return jnp.concatenate([top, bot], axis=-2)


def _from_blockdiag(sbd):
    B, P = sbd.shape[:2]
    return jnp.stack([sbd[:, :, :HEAD, :HEAD], sbd[:, :, HEAD:, HEAD:]], axis=2).reshape(B, 2 * P, HEAD, HEAD)


def _qknorm_kernel(q_ref, k_ref, v_ref, c_ref, s1_ref, s2_ref, qg_ref, kg_ref,
                   qb_o, kf_o, kb_o, vf_o, vb_o):
    cos, s1, s2 = c_ref[...], s1_ref[...], s2_ref[...]
    ones2 = _head_ones()

    def norm_rope(x, gain):
        ms = _head_allsum(x * x, ones2) * (1.0 / HEAD)
        y = x * lax.rsqrt(ms + RMS_EPS) * gain
        half = ROT_DIM // 2
        return y * cos + pltpu.roll(y, LANES - half, axis=1) * s1 + pltpu.roll(y, half, axis=1) * s2

    for c in range(0, q_ref.shape[1], LANES):
        cs = slice(c, c + LANES)
        q = norm_rope(q_ref[:, cs], qg_ref[...])
        k = norm_rope(k_ref[:, cs], kg_ref[...])
        qb_o[:, cs] = (q * (HEAD ** -0.5)).astype(qb_o.dtype)
        kf_o[:, cs] = k
        kb_o[:, cs] = k.astype(kb_o.dtype)
    v = v_ref[...]
    vf_o[...] = v
    vb_o[...] = v.astype(vb_o.dtype)


def _rope_tables(pos):
    half = ROT_DIM // 2
    inv_freq = jnp.power(jnp.float32(ROPE_THETA), -jnp.arange(0, ROT_DIM, 2, dtype=jnp.float32) / ROT_DIM)
    ang = pos.astype(jnp.float32)[:, None] * inv_freq[None, :]
    cos, sin = jnp.cos(ang), jnp.sin(ang)
    T = pos.shape[0]
    pad = jnp.zeros((T, HEAD - ROT_DIM), F32)
    z = jnp.zeros((T, half), F32)
    c = jnp.concatenate([cos, cos, pad + 1.0], axis=1)
    s1 = jnp.concatenate([-sin, z, pad], axis=1)
    s2 = jnp.concatenate([z, sin, pad], axis=1)
    tile = lambda t: jnp.concatenate([t, t], axis=1)
    return tile(c), tile(s1), tile(s2)


def _qknorm(proj, row_off, rows, DW, tables, q_norm_g, k_norm_g):
    tr = _pick(rows, 256, 16)
    cw = _pick(DW, 512, LANES)
    assert row_off % tr == 0
    r0 = row_off // tr
    nc = DW // cw
    gain = lambda g: jnp.concatenate([g, g]).reshape(1, LANES)
    seg = lambda s: pl.BlockSpec((tr, cw), lambda i, j: (r0 + i, s * nc + j))
    tab = pl.BlockSpec((tr, LANES), lambda i, j: (i, 0))
    vec = pl.BlockSpec((1, LANES), lambda i, j: (0, 0))
    out = pl.BlockSpec((tr, cw), lambda i, j: (i, j))
    sds = lambda dt: jax.ShapeDtypeStruct((rows, DW), dt)
    return pl.pallas_call(
        _qknorm_kernel, grid=(rows // tr, nc),
        in_specs=[seg(0), seg(1), seg(2), tab, tab, tab, vec, vec],
        out_specs=[out] * 5,
        out_shape=[sds(BF16), sds(F32), sds(BF16), sds(F32), sds(BF16)],
        compiler_params=_cp(("parallel", "parallel")), name="qk_norm_rope",
    )(proj, proj, proj, *tables, gain(q_norm_g), gain(k_norm_g))


def _lambda(lam_ref, lam_init):
    lam = lam_ref[...]
    return (jnp.exp(jnp.sum(lam[0:1] * lam[1:2], axis=1, keepdims=True))
            - jnp.exp(jnp.sum(lam[2:3] * lam[3:4], axis=1, keepdims=True)) + lam_init)


def _stack_components(q):
    lane = lax.broadcasted_iota(jnp.int32, q.shape, 1)
    zero = jnp.zeros_like(q)
    return jnp.concatenate([jnp.where(lane < HEAD, q, zero), jnp.where(lane >= HEAD, q, zero)], axis=0)


def _softmax_step(qq, k, v, m_ref, l_ref, acc_ref, mask):
    s = lax.dot_general(qq, k, (((2,), (2,)), ((0,), (0,))), preferred_element_type=F32)
    if mask is not None:
        s = jnp.where(mask, s, NEG_INF)
    m_old = m_ref[...]
    m_new = jnp.maximum(m_old, jnp.max(s, axis=2, keepdims=True))
    alpha = jnp.exp(m_old - m_new)
    tk = s.shape[2]
    p = jnp.exp(s - jnp.concatenate([m_new] * pl.cdiv(tk, LANES), axis=2)[:, :, :tk])
    v1 = jnp.concatenate([v, jnp.ones(v.shape, v.dtype)], axis=2)
    pv = lax.dot_general(p.astype(BF16), v1, (((2,), (1,)), ((0,), (0,))), preferred_element_type=F32)
    l_ref[...] = alpha * l_ref[...] + pv[:, :, LANES:]
    acc_ref[...] = alpha * acc_ref[...] + pv[:, :, :LANES]
    m_ref[...] = m_new


def _finish_head(t, m_ref, l_ref, acc_ref, lam, g, lam_init):
    o = acc_ref[...] / l_ref[...]
    o = o[:t] - lam * o[t:]
    ms = jnp.mean(o * o, axis=-1, keepdims=True)
    return o * lax.rsqrt(ms + RMS_EPS) * g * (1.0 - lam_init)


def _attn_prompt_kernel(lam_ref, g_ref, q_ref, k_ref, v_ref, o_ref, m_ref, l_ref, acc_ref, *, lam_init):
    qi = pl.program_id(2)
    tq = q_ref.shape[0]
    nh = q_ref.shape[1] // LANES
    cols = [slice(h * LANES, (h + 1) * LANES) for h in range(nh)]
    qq = jnp.stack([_stack_components(q_ref[:, cs]) for cs in cols])
    m_ref[...] = jnp.full(m_ref.shape, NEG_INF, F32)
    l_ref[...] = jnp.zeros(l_ref.shape, F32)
    acc_ref[...] = jnp.zeros(acc_ref.shape, F32)

    def block(j, mask):
        rows = pl.ds(pl.multiple_of(j * tq, tq), tq)
        _softmax_step(qq, jnp.stack([k_ref[rows, cs] for cs in cols]), jnp.stack([v_ref[rows, cs] for cs in cols]),
                      m_ref, l_ref, acc_ref, mask)

    def body(j, carry):
        block(j, None)
        return carry

    lax.fori_loop(0, qi, body, 0)
    r = lax.broadcasted_iota(jnp.int32, (2 * tq, tq), 0)
    cidx = lax.broadcasted_iota(jnp.int32, (2 * tq, tq), 1)
    r = jnp.where(r >= tq, r - tq, r)
    block(qi, (r // CHUNK) >= (cidx // CHUNK))
    lam = _lambda(lam_ref, lam_init)
    for h, cs in enumerate(cols):
        o_ref[:, cs] = _finish_head(tq, m_ref.at[h], l_ref.at[h], acc_ref.at[h], lam, g_ref[...],
                                    lam_init).astype(o_ref.dtype)


def _attn_prompt(qb, kb, vb, lam_vecs, subln_g, B, T, lam_init):
    DW = qb.shape[1]
    nh = DW // LANES
    hg = math.gcd(nh, ATTN_HEADS)
    tq = _pick(T, 256, CHUNK)
    nq = T // tq
    return pl.pallas_call(
        functools.partial(_attn_prompt_kernel, lam_init=lam_init), grid=(B, nh // hg, nq),
        in_specs=[pl.BlockSpec(lam_vecs.shape, lambda b, h, i: (0, 0)),
                  pl.BlockSpec((1, LANES), lambda b, h, i: (0, 0)),
                  pl.BlockSpec((tq, hg * LANES), lambda b, h, i: (b * nq + i, h)),
                  pl.BlockSpec((T, hg * LANES), lambda b, h, i: (b, h)),
                  pl.BlockSpec((T, hg * LANES), lambda b, h, i: (b, h))],
        out_specs=pl.BlockSpec((tq, hg * LANES), lambda b, h, i: (b * nq + i, h)),
        out_shape=jax.ShapeDtypeStruct((B * T, DW), BF16),
        scratch_shapes=[pltpu.VMEM((hg, 2 * tq, LANES), F32)] * 3,
        compiler_params=_cp(("parallel", "parallel", "arbitrary")), name="attn_prompt",
    )(lam_vecs, subln_g.reshape(1, LANES), qb, kb, vb)


def _attn_sample_kernel(lam_ref, g_ref, q_ref, kc_ref, vc_ref, kn_ref, vn_ref, o_ref, m_ref, l_ref, acc_ref,
                        *, lam_init, past):
    j = pl.program_id(1)
    ts = q_ref.shape[0]
    nh = q_ref.shape[1] // LANES

    @pl.when(j == 0)
    def _():
        m_ref[...] = jnp.full(m_ref.shape, NEG_INF, F32)
        l_ref[...] = jnp.zeros(l_ref.shape, F32)
        acc_ref[...] = jnp.zeros(acc_ref.shape, F32)

    cols = [slice(h * LANES, (h + 1) * LANES) for h in range(nh)]
    tk = kc_ref.shape[0] // nh

    def heads(head_k, head_v, mask):
        qq = jnp.stack([_stack_components(q_ref[:, cs]) for cs in cols])
        _softmax_step(qq, jnp.stack([head_k(h).astype(BF16) for h in range(nh)]),
                      jnp.stack([head_v(h).astype(BF16) for h in range(nh)]), m_ref, l_ref, acc_ref, mask)

    heads(lambda h: kc_ref[pl.ds(h, tk, stride=nh), :], lambda h: vc_ref[pl.ds(h, tk, stride=nh), :], None)

    @pl.when(j == pl.num_programs(1) - 1)
    def _():
        r = lax.broadcasted_iota(jnp.int32, (2 * ts, ts), 0)
        cidx = lax.broadcasted_iota(jnp.int32, (2 * ts, ts), 1)
        r = jnp.where(r >= ts, r - ts, r)
        heads(lambda h: kn_ref[:, cols[h]], lambda h: vn_ref[:, cols[h]],
              ((past + r) // CHUNK) >= ((past + cidx) // CHUNK))
        lam = _lambda(lam_ref, lam_init)
        for h in range(nh):
            cs = slice(h * LANES, (h + 1) * LANES)
            o_ref[:, cs] = _finish_head(ts, m_ref.at[h], l_ref.at[h], acc_ref.at[h], lam, g_ref[...],
                                        lam_init).astype(o_ref.dtype)


def _attn_sample(qb, kb, vb, cache_k, cache_v, lam_vecs, subln_g, B, Ts, lam_init):
    DW = qb.shape[1]
    nh = DW // LANES
    past = cache_k.shape[0] // (B * nh)
    tk = _pick(past, 512, 16)
    nkv = past // tk
    new = pl.BlockSpec((Ts, DW), lambda b, j: (b, 0))
    old = pl.BlockSpec((tk * nh, LANES), lambda b, j: (b * nkv + j, 0))
    return pl.pallas_call(
        functools.partial(_attn_sample_kernel, lam_init=lam_init, past=past), grid=(B, nkv),
        in_specs=[pl.BlockSpec(lam_vecs.shape, lambda b, j: (0, 0)),
                  pl.BlockSpec((1, LANES), lambda b, j: (0, 0)),
                  new, old, old, new, new],
        out_specs=new,
        out_shape=jax.ShapeDtypeStruct((B * Ts, DW), BF16),
        scratch_shapes=[pltpu.VMEM((nh, 2 * Ts, LANES), F32)] * 3,
        compiler_params=_cp(("parallel", "arbitrary")), name="attn_sample",
    )(lam_vecs, subln_g.reshape(1, LANES), qb, cache_k, cache_v, kb, vb)


def _layer(i, x_prompt, x_sample, pe_prompt, pe_sample, cache_k, cache_v, state_rwkv, state_shift, w):
    Bp, Tp, D = x_prompt.shape
    Bs, Ts, _ = x_sample.shape
    Mp, Ms = Bp * Tp, Bs * Ts
    past = cache_k.shape[1]
    RW = w["w0"].shape[-1]
    DW = w["w_branch_diff"].shape[0]
    SW = state_shift.shape[-1]
    SWp = _round_up(SW, SEG_ALIGN)
    w_in = w["w_in"]
    lam_init = 0.8 - 0.6 * math.exp(-0.3 * i)

    x_all, h1 = _rmsnorm_cat(x_prompt.reshape(Mp, D), x_sample.reshape(Ms, D), w["norm1_g"])
    w_t = w_in.T
    proj_r = _mm_nt_rows(h1, w_t, 0, SWp, name="proj_rwkv")
    proj_a = _mm_nt_rows(h1, w_t, SW, w_t.shape[0] - SW, name="proj_attn")

    pad_sw = lambda a: jnp.pad(a, ((0, 0), (0, 0), (0, SWp - SW)))
    mu = jnp.pad(w["tm_mu"], (0, SWp - SW))
    lam_vecs = jnp.stack([w["lam_q1"], w["lam_k1"], w["lam_q2"], w["lam_k2"]])
    rwkv_w = (mu, w["w0"], w["a0"], w["k_k"], w["k_a"], w["r_k"].reshape(-1), w["w_decay_up"], w["a_up"], w["g_up"],
              w["lnx_g"], w["lnx_b"])

    s0_p = jnp.zeros((Bp, RW // LANES, LANES, LANES), F32)
    yr_p, s_p, last_p = _rwkv(proj_r, 0, Bp, Tp, SWp, jnp.zeros((Bp, 1, SWp), F32), s0_p, *rwkv_w)
    tab_p = _rope_tables(jnp.arange(Tp, dtype=jnp.int32))
    tab_p = tuple(jnp.tile(t, (Bp, 1)) for t in tab_p)
    qb_p, kf_p, kb_p, vf_p, vb_p = _qknorm(proj_a, 0, Mp, DW, tab_p, w["q_norm_g"], w["k_norm_g"])
    yd_p = _attn_prompt(qb_p, kb_p, vb_p, lam_vecs, w["subln_g"], Bp, Tp, lam_init)

    yr_s, s_s, last_s = _rwkv(proj_r, Mp, Bs, Ts, SWp, pad_sw(state_shift), _to_blockdiag(state_rwkv), *rwkv_w)
    tab_s = _rope_tables(past + jnp.arange(Ts, dtype=jnp.int32))
    tab_s = tuple(jnp.tile(t, (Bs, 1)) for t in tab_s)
    qb_s, kf_s, kb_s, vf_s, vb_s = _qknorm(proj_a, Mp, Ms, DW, tab_s, w["q_norm_g"], w["k_norm_g"])
    yd_s = _attn_sample(qb_s, kb_s, vb_s, cache_k.reshape(-1, LANES), cache_v.reshape(-1, LANES),
                        lam_vecs, w["subln_g"], Bs, Ts, lam_init)

    y_r = jnp.concatenate([yr_p, yr_s], axis=0)
    y_d = jnp.concatenate([yd_p, yd_s], axis=0)
    merged = _merge(y_r, y_d, w["w_branch_rwkv"], w["w_branch_diff"], proj_a, 3 * DW)
    x1 = _mm_fullk(_mm_res_kernel, merged, w["w_out"], F32, res=x_all, tn_target=256, name="out_proj")
    h2 = _rmsnorm(x1, w["norm2_g"])
    u = _mm_fullk(_mm_relu2_kernel, h2, w["w_up"], BF16, name="ffn_up")
    x2 = _mm_res_kloop(u, w["w_down"], x1, name="ffn_down")
    h3 = _rmsnorm(x2, w["ple_norm_g"])
    y_p = _ple(h3, x2, pe_prompt.reshape(Mp, -1), w["w_ple_gate"], w["w_ple_proj"], 0, Mp)
    y_s = _ple(h3, x2, pe_sample.reshape(Ms, -1), w["w_ple_gate"], w["w_ple_proj"], Mp, Ms)

    nh_d = DW // LANES
    shift_p = last_p[:, :SW].reshape(Bp, 1, SW)
    shift_s = last_s[:, :SW].reshape(Bs, 1, SW)
    return (y_p.reshape(Bp, Tp, D), y_s.reshape(Bs, Ts, D),
            kf_p.reshape(Bp, Tp, nh_d, LANES), vf_p.reshape(Bp, Tp, nh_d, LANES), _from_blockdiag(s_p), shift_p,
            kf_s.reshape(Bs, Ts, nh_d, LANES), vf_s.reshape(Bs, Ts, nh_d, LANES), _from_blockdiag(s_s), shift_s)


def kernel(x_prompt, x_sample, p_prompt, p_sample, cache_k, cache_v, state_rwkv, state_shift, norm1_g, w_in, tm_mu, w_decay_up, w0, a_up, a0, g_up, k_k, k_a, r_k, lnx_g, lnx_b, q_norm_g, k_norm_g, lam_q1, lam_k1, lam_q2, lam_k2, subln_g, w_branch_rwkv, w_branch_diff, w_out, norm2_g, w_up, w_down, ple_norm_g, w_ple_gate, w_ple_proj):
    weights = dict(norm1_g=norm1_g, w_in=w_in, tm_mu=tm_mu, w_decay_up=w_decay_up, w0=w0, a_up=a_up, a0=a0,
                   g_up=g_up, k_k=k_k, k_a=k_a, r_k=r_k, lnx_g=lnx_g, lnx_b=lnx_b, q_norm_g=q_norm_g,
                   k_norm_g=k_norm_g, lam_q1=lam_q1, lam_k1=lam_k1, lam_q2=lam_q2, lam_k2=lam_k2,
                   subln_g=subln_g, w_branch_rwkv=w_branch_rwkv, w_branch_diff=w_branch_diff, w_out=w_out,
                   norm2_g=norm2_g, w_up=w_up, w_down=w_down, ple_norm_g=ple_norm_g, w_ple_gate=w_ple_gate,
                   w_ple_proj=w_ple_proj)
    depth = w_in.shape[0]
    hp, hs = x_prompt, x_sample
    outs = [[] for _ in range(8)]
    for i in range(depth):
        res = _layer(i, hp, hs, p_prompt[i], p_sample[i], cache_k[i], cache_v[i], state_rwkv[i], state_shift[i],
                     {name: val[i] for name, val in weights.items()})
        hp, hs = res[0], res[1]
        for acc, val in zip(outs, res[2:]):
            acc.append(val)
    return (hp, hs) + tuple(jnp.stack(acc, 0) for acc in outs)
```

```python
import functools
import math

import jax
import jax.numpy as jnp
from jax import lax
from jax.experimental import pallas as pl
from jax.experimental.pallas import tpu as pltpu

F32 = jnp.float32
BF16 = jnp.bfloat16

LANES = 128
HEAD = 64
RMS_EPS = 1e-6
GN_EPS = HEAD * 1e-5
NEG_INF = -1e30
CHUNK = 64
ROT_DIM = HEAD // 4
ROPE_THETA = 500000.0
SCAN_L = 128
SCAN_PAIRS = 8
ATTN_HEADS = 8
SEG_ALIGN = 512
VMEM_LIMIT = 56 * 1024 * 1024
NT = (((1,), (1,)), ((), ()))


def _round_up(n, m):
    return (n + m - 1) // m * m


def _pick(n, target, mult):
    best = None
    for d in range(mult, min(n, target) + 1, mult):
        if n % d == 0:
            best = d
    assert best is not None, (n, target, mult)
    return best


def _cp(sem):
    return pltpu.CompilerParams(dimension_semantics=sem, vmem_limit_bytes=VMEM_LIMIT)


def _bdot(a, b):
    return jnp.dot(a.astype(BF16), b.astype(BF16), preferred_element_type=F32)


def _bdot_nt(a, b):
    return lax.dot_general(a.astype(BF16), b.astype(BF16), NT, preferred_element_type=F32)


def _head_ones():
    r = lax.broadcasted_iota(jnp.int32, (2 * LANES, LANES), 0)
    c = lax.broadcasted_iota(jnp.int32, (2 * LANES, LANES), 1)
    r = jnp.where(r >= LANES, r - LANES, r)
    return jnp.where((r >= HEAD) == (c >= HEAD), 1.0, 0.0).astype(BF16)


def _head_allsum(x, ones2):
    hi = x.astype(BF16)
    lo = (x - hi.astype(F32)).astype(BF16)
    return jnp.dot(jnp.concatenate([hi, lo], axis=1), ones2, preferred_element_type=F32)


def _rmsnorm_kernel(x_ref, g_ref, o_ref):
    x = x_ref[...]
    ms = jnp.mean(x * x, axis=-1, keepdims=True)
    o_ref[...] = (x * lax.rsqrt(ms + RMS_EPS) * g_ref[...]).astype(o_ref.dtype)


def _rmsnorm(x, g):
    M, D = x.shape
    tr = _pick(M, 256, 16)
    return pl.pallas_call(
        _rmsnorm_kernel, grid=(M // tr,),
        in_specs=[pl.BlockSpec((tr, D), lambda i: (i, 0)), pl.BlockSpec((1, D), lambda i: (0, 0))],
        out_specs=pl.BlockSpec((tr, D), lambda i: (i, 0)),
        out_shape=jax.ShapeDtypeStruct((M, D), BF16),
        compiler_params=_cp(("parallel",)), name="rmsnorm")(x, g.reshape(1, D))


def _rmsnorm_cat_kernel(xp_ref, xs_ref, g_ref, x_o, h_o, *, n_p):
    x = jnp.where(pl.program_id(0) < n_p, xp_ref[...], xs_ref[...])
    ms = jnp.mean(x * x, axis=-1, keepdims=True)
    x_o[...] = x
    h_o[...] = (x * lax.rsqrt(ms + RMS_EPS) * g_ref[...]).astype(h_o.dtype)


def _rmsnorm_cat(xp, xs, g):
    Mp, D = xp.shape
    Ms = xs.shape[0]
    tr = _pick(math.gcd(Mp, Ms), 256, 16)
    n_p = Mp // tr
    blk = pl.BlockSpec((tr, D), lambda i: (i, 0))
    return pl.pallas_call(
        functools.partial(_rmsnorm_cat_kernel, n_p=n_p), grid=((Mp + Ms) // tr,),
        in_specs=[pl.BlockSpec((tr, D), lambda i: (jnp.minimum(i, n_p - 1), 0)),
                  pl.BlockSpec((tr, D), lambda i: (jnp.maximum(i - n_p, 0), 0)),
                  pl.BlockSpec((1, D), lambda i: (0, 0))],
        out_specs=[blk, blk],
        out_shape=[jax.ShapeDtypeStruct((Mp + Ms, D), F32), jax.ShapeDtypeStruct((Mp + Ms, D), BF16)],
        compiler_params=_cp(("parallel",)), name="rmsnorm_cat")(xp, xs, g.reshape(1, D))


def _mm_relu2_kernel(a_ref, b_ref, o_ref):
    p = jnp.maximum(_bdot(a_ref[...], b_ref[...]), 0.0)
    o_ref[...] = (p * p).astype(o_ref.dtype)


def _mm_res_kernel(a_ref, b_ref, r_ref, o_ref):
    o_ref[...] = r_ref[...] + _bdot(a_ref[...], b_ref[...])


def _mm_fullk(kernel, a, b, out_dtype, res=None, tm_target=1408, tn_target=512, name="mm"):
    M, K = a.shape
    N = b.shape[1]
    tm = _pick(M, tm_target, 16)
    tn = _pick(N, tn_target, LANES)
    in_specs = [pl.BlockSpec((tm, K), lambda i, j: (i, 0)), pl.BlockSpec((K, tn), lambda i, j: (0, j))]
    args = [a, b]
    if res is not None:
        in_specs.append(pl.BlockSpec((tm, tn), lambda i, j: (i, j)))
        args.append(res)
    return pl.pallas_call(
        kernel, grid=(M // tm, N // tn), in_specs=in_specs,
        out_specs=pl.BlockSpec((tm, tn), lambda i, j: (i, j)),
        out_shape=jax.ShapeDtypeStruct((M, N), out_dtype),
        compiler_params=_cp(("parallel", "parallel")), name=name)(*args)


def _mm_nt_kernel(a_ref, bt_ref, o_ref):
    o_ref[...] = _bdot_nt(a_ref[...], bt_ref[...])


def _mm_nt_rows(a, bt, row0, n_rows, name):
    M, K = a.shape
    tm = _pick(M, 1408, 16)
    tn = _pick(n_rows, 512, LANES)
    assert row0 % 8 == 0
    return pl.pallas_call(
        _mm_nt_kernel, grid=(M // tm, n_rows // tn),
        in_specs=[pl.BlockSpec((tm, K), lambda i, j: (i, 0)),
                  pl.BlockSpec((pl.Element(tn), pl.Element(K)),
                               lambda i, j: ((row0 // 8 + j * (tn // 8)) * 8, 0))],
        out_specs=pl.BlockSpec((tm, tn), lambda i, j: (i, j)),
        out_shape=jax.ShapeDtypeStruct((M, n_rows), F32),
        compiler_params=_cp(("parallel", "parallel")), name=name)(a, bt)


def _mm_res_kloop_kernel(a_ref, b_ref, r_ref, o_ref):
    @pl.when(pl.program_id(2) == 0)
    def _():
        o_ref[...] = r_ref[...]

    o_ref[...] += _bdot(a_ref[...], b_ref[...])


def _mm_res_kloop(a, b, res, name):
    M, K = a.shape
    N = b.shape[1]
    tm = _pick(M, 1408, 16)
    tn = _pick(N, 1024, LANES)
    tk = _pick(K, 1024, LANES)
    return pl.pallas_call(
        _mm_res_kloop_kernel, grid=(M // tm, N // tn, K // tk),
        in_specs=[pl.BlockSpec((tm, tk), lambda i, j, k: (i, k)),
                  pl.BlockSpec((tk, tn), lambda i, j, k: (k, j)),
                  pl.BlockSpec((tm, tn), lambda i, j, k: (i, j))],
        out_specs=pl.BlockSpec((tm, tn), lambda i, j, k: (i, j)),
        out_shape=jax.ShapeDtypeStruct((M, N), F32),
        compiler_params=_cp(("parallel", "parallel", "arbitrary")), name=name)(a, b, res)


def _merge_kernel(yr_ref, yd_ref, wr_ref, wd_ref, gr_ref, gd_ref, o_ref):
    pr = _bdot(yr_ref[...], wr_ref[...])
    pd = _bdot(yd_ref[...], wd_ref[...])
    o_ref[...] = (jax.nn.sigmoid(gr_ref[...]) * pr + jax.nn.sigmoid(gd_ref[...]) * pd).astype(o_ref.dtype)


def _merge(y_r, y_d, w_r, w_d, proj, gate_off):
    M, Kr = y_r.shape
    Kd = y_d.shape[1]
    D = w_r.shape[1]
    tm = _pick(M, 1408, 16)
    tn = _pick(D, 256, LANES)
    assert gate_off % tn == 0
    g0 = gate_off // tn
    g1 = (gate_off + D) // tn
    return pl.pallas_call(
        _merge_kernel, grid=(M // tm, D // tn),
        in_specs=[pl.BlockSpec((tm, Kr), lambda i, j: (i, 0)),
                  pl.BlockSpec((tm, Kd), lambda i, j: (i, 0)),
                  pl.BlockSpec((Kr, tn), lambda i, j: (0, j)),
                  pl.BlockSpec((Kd, tn), lambda i, j: (0, j)),
                  pl.BlockSpec((tm, tn), lambda i, j: (i, g0 + j)),
                  pl.BlockSpec((tm, tn), lambda i, j: (i, g1 + j))],
        out_specs=pl.BlockSpec((tm, tn), lambda i, j: (i, j)),
        out_shape=jax.ShapeDtypeStruct((M, D), BF16),
        compiler_params=_cp(("parallel", "parallel")), name="branch_merge")(y_r, y_d, w_r, w_d, proj, proj)


def _ple_kernel(h_ref, wg_ref, pe_ref, wp_ref, x_ref, o_ref):
    gate = jax.nn.sigmoid(_bdot(h_ref[...], wg_ref[...]))
    o_ref[...] = x_ref[...] + gate * _bdot(pe_ref[...], wp_ref[...])


def _ple(h, x, pe, w_gate, w_proj, row_off, rows):
    D = x.shape[1]
    P = pe.shape[1]
    tm = _pick(rows, 1024, 16)
    tn = _pick(D, 512, LANES)
    assert row_off % tm == 0
    r0 = row_off // tm
    return pl.pallas_call(
        _ple_kernel, grid=(rows // tm, D // tn),
        in_specs=[pl.BlockSpec((tm, D), lambda i, j: (r0 + i, 0)),
                  pl.BlockSpec((D, tn), lambda i, j: (0, j)),
                  pl.BlockSpec((tm, P), lambda i, j: (i, 0)),
                  pl.BlockSpec((P, tn), lambda i, j: (0, j)),
                  pl.BlockSpec((tm, tn), lambda i, j: (r0 + i, j))],
        out_specs=pl.BlockSpec((tm, tn), lambda i, j: (i, j)),
        out_shape=jax.ShapeDtypeStruct((rows, D), F32),
        compiler_params=_cp(("parallel", "parallel")), name="ple")(h, w_gate, pe, w_proj, x)


def _head_allsum_wide(x, ones2):
    n = x.shape[1] // LANES
    rows = x.shape[0]
    s = _head_allsum(jnp.concatenate([x[:, i * LANES:(i + 1) * LANES] for i in range(n)], axis=0), ones2)
    return jnp.concatenate([s[i * rows:(i + 1) * rows] for i in range(n)], axis=1)


def _rwkv_kernel(pr_ref, pk_ref, pv_ref, pw_ref, spr_ref, spk_ref, spv_ref, spw_ref,
                 mur_ref, muk_ref, muv_ref, muw_ref, w0_ref, a0_ref, kk_ref, ka_ref, rk_ref,
                 wd_ref, wa_ref, wg_ref, lng_ref, lnb_ref, s0_ref,
                 y_ref, so_ref, last_ref, sbd_ref, prev_ref, *, PB, L, win_w, win_a, win_g):
    c = pl.program_id(2)
    Tb = pr_ref.shape[0]
    W = PB * LANES
    lane = lax.broadcasted_iota(jnp.int32, (L, LANES), 1)
    lane2 = lax.broadcasted_iota(jnp.int32, (2 * L, LANES), 1)
    row = lax.broadcasted_iota(jnp.int32, (L, L), 0)
    col = lax.broadcasted_iota(jnp.int32, (L, L), 1)
    incl = row >= col
    strict = row > col
    tri = jnp.where(incl, 1.0, 0.0).astype(BF16)
    eye = jnp.where(row == col, 1.0, 0.0)
    srow = lax.broadcasted_iota(jnp.int32, (LANES, LANES), 0)
    scol = lax.broadcasted_iota(jnp.int32, (LANES, LANES), 1)
    same_head = (srow >= HEAD) == (scol >= HEAD)
    ones2 = _head_ones()

    @pl.when(c == 0)
    def _():
        sbd_ref[...] = s0_ref[0]
        prev_ref[:, 0:W] = spr_ref[0]
        prev_ref[:, W:2 * W] = spk_ref[0]
        prev_ref[:, 2 * W:3 * W] = spv_ref[0]
        prev_ref[:, 3 * W:] = spw_ref[0]

    first_row = lax.broadcasted_iota(jnp.int32, (Tb, 1), 0) == 0

    def shifted(x_ref, mu_ref, off):
        x = x_ref[...]
        prev = prev_ref[:, off:off + x.shape[1]]
        prev_ref[:, off:off + x.shape[1]] = x[Tb - 1:Tb, :]
        sh = jnp.where(first_row, prev, pltpu.roll(x, 1, axis=0))
        return x + (sh - x) * mu_ref[...]

    r = shifted(pr_ref, mur_ref, 0)
    k = shifted(pk_ref, muk_ref, W)
    v = shifted(pv_ref, muv_ref, 2 * W)
    lo = shifted(pw_ref, muw_ref, 3 * W)
    z = w0_ref[...] + _bdot(jnp.tanh(lo[:, win_w[0]:win_w[1]]), wd_ref[...])
    lw = -math.exp(-0.5) * jax.nn.sigmoid(z)
    alr = jax.nn.sigmoid(a0_ref[...] + _bdot(lo[:, win_a[0]:win_a[1]], wa_ref[...]))
    gate = _bdot(jax.nn.sigmoid(lo[:, win_g[0]:win_g[1]]), wg_ref[...])
    kk = k * kk_ref[...]
    k2 = k * (1.0 + (alr - 1.0) * ka_ref[...])
    kkn = kk * lax.rsqrt(_head_allsum_wide(kk * kk, ones2) + 1e-12)
    bonus_gate = _head_allsum_wide(r * k2 * rk_ref[...], ones2) * v * gate
    if Tb < L:
        pad = lambda x: jnp.concatenate([x, jnp.zeros((L - Tb, W), F32)], axis=0)
        r, k2, v, lw, kkn, alr = pad(r), pad(k2), pad(v), pad(lw), pad(kkn), pad(alr)

    pairs = range(PB)
    heads = [(p, e) for p in pairs for e in range(2)]
    cols = [slice(p * LANES, (p + 1) * LANES) for p in pairs]
    bmm = lambda x, y: lax.dot_general(x.astype(BF16), y.astype(BF16), (((2,), (1,)), ((0,), (0,))),
                                       preferred_element_type=F32)
    bmm_nt = lambda x, y: lax.dot_general(x.astype(BF16), y.astype(BF16), (((2,), (2,)), ((0,), (0,))),
                                          preferred_element_type=F32)
    rows_cat = lambda xs: jnp.concatenate(xs, axis=0)

    w1 = lw.astype(BF16)
    res1 = lw - w1.astype(F32)
    w2 = res1.astype(BF16)
    w3 = (res1 - w2.astype(F32)).astype(BF16)
    lc3 = jnp.dot(tri, jnp.concatenate([w1, w2, w3], axis=1), preferred_element_type=F32)
    lc = lc3[:, :W] + lc3[:, W:2 * W] + lc3[:, 2 * W:]
    lc_last = lc[L - 1:L, :]
    g_in, g_prev, g_inv = jnp.exp(lc), jnp.exp(lc - lw), jnp.exp(-lc)
    ratio = jnp.exp(lc_last - lc)
    g_last = jnp.exp(lc_last)
    b = kkn * alr
    At, Rt, Bt, Kt = -kkn * g_prev, r * g_in, b * g_inv, k2 * g_inv
    Bh, Kh = b * ratio, k2 * ratio

    he = lane >= HEAD
    own = [lane2 < HEAD, lane2 >= HEAD]
    X = [rows_cat([At[:, cs], Rt[:, cs]]) for cs in cols]
    Y = [rows_cat([Bt[:, cs], Kt[:, cs]]) for cs in cols]
    G = bmm_nt(jnp.stack([jnp.where(own[e], X[p], 0.0) for p, e in heads]),
               jnp.stack([Y[p] for p, e in heads]))
    Nab = jnp.where(strict, G[:, :L, :L], 0.0)
    Aak = jnp.where(strict, G[:, :L, L:], 0.0)
    Arb = jnp.where(incl, G[:, L:, :L], 0.0)
    Ark = jnp.where(incl, G[:, L:, L:], 0.0)
    Tm = eye + Nab
    Np = Nab
    lev = 1
    while lev * 2 < L:
        Np = bmm(Np, Np)
        Tm = Tm + bmm(Tm, Np)
        lev *= 2
    AV = bmm(jnp.concatenate([Aak, Ark], axis=1), jnp.stack([v[:, cols[p]] for p, e in heads]))
    TW = bmm(Tm, jnp.concatenate([jnp.stack([At[:, cols[p]] for p, e in heads]), AV[:, :L]], axis=2))
    pick = lambda x, p: jnp.where(he, x[2 * p + 1], x[2 * p])
    W1 = [pick(TW[:, :, :LANES], p) for p in pairs]
    W2 = [pick(TW[:, :, LANES:], p) for p in pairs]
    Op = [pick(AV[:, L:], p) for p in pairs]
    S = sbd_ref[...]
    XS = bmm_nt(jnp.stack([rows_cat([W1[p], Rt[:, cols[p]]]) for p in pairs]), S)
    U = [XS[p, :L] + W2[p] for p in pairs]
    AU = bmm(Arb, jnp.stack([U[p] for p, e in heads]))
    O = [XS[p, L:] + Op[p] + pick(AU, p) for p in pairs]
    upd = bmm(jnp.stack([rows_cat([U[p], v[:, cols[p]]]).T for p in pairs]),
              jnp.stack([rows_cat([Bh[:, cols[p]], Kh[:, cols[p]]]) for p in pairs]))
    sbd_ref[...] = S * jnp.stack([g_last[:, cols[p]] for p in pairs]) + jnp.where(same_head, upd, 0.0)
    Oc = rows_cat(O)
    mu = _head_allsum(Oc, ones2) * (1.0 / HEAD)
    d = Oc - mu
    var = _head_allsum(d * d, ones2) * (1.0 / HEAD)
    nrm = d * lax.rsqrt(var + GN_EPS)
    for p in pairs:
        gn = nrm[p * L:p * L + Tb] * lng_ref[:, cols[p]] + lnb_ref[:, cols[p]]
        y_ref[:, cols[p]] = (gn * gate[:, cols[p]] + bonus_gate[:, cols[p]]).astype(y_ref.dtype)

    @pl.when(c == pl.num_programs(2) - 1)
    def _():
        so_ref[0] = sbd_ref[...]
        last_ref[0, 0] = prev_ref[...]


def _lora_window(start, width, w, base):
    s0 = start // LANES * LANES
    s1 = _round_up(start + width, LANES)
    wp = jnp.zeros((s1 - s0, w.shape[1]), w.dtype).at[start - s0:start - s0 + width].set(w)
    return (s0 - base, s1 - base), wp


def _rwkv(proj, row_off, B, T, SWp, shift_prev, s0_bd, mu, w0, a0, k_k, k_a, r_k, w_decay_up, a_up, g_up,
          lnx_g, lnx_b):
    RW = w0.shape[-1]
    L = SCAN_L
    Tb = min(T, L)
    assert T % Tb == 0 and row_off % Tb == 0
    nc = T // Tb
    r0 = row_off // Tb
    npair = RW // LANES
    PB = math.gcd(npair, SCAN_PAIRS)
    W = PB * LANES
    ng = RW // W
    LW = SWp - 3 * RW
    assert LW % LANES == 0 and (3 * RW) % LW == 0
    lb = 3 * RW // LW
    DL, AL, GL = w_decay_up.shape[0], a_up.shape[0], g_up.shape[0]
    win_w, wd = _lora_window(3 * RW, DL, w_decay_up, 3 * RW)
    win_a, wa = _lora_window(3 * RW + DL, AL, a_up, 3 * RW)
    win_g, wg = _lora_window(3 * RW + DL + AL, GL, g_up, 3 * RW)
    row = lambda x: x.reshape(1, -1)
    seg = lambda s: pl.BlockSpec((Tb, W), lambda b, g, c: (r0 + b * nc + c, s * ng + g))
    seg_prev = lambda s: pl.BlockSpec((1, 1, W), lambda b, g, c: (b, 0, s * ng + g))
    seg_mu = lambda s: pl.BlockSpec((1, W), lambda b, g, c: (0, s * ng + g))
    vec = pl.BlockSpec((1, W), lambda b, g, c: (0, g))
    wblk = lambda w: pl.BlockSpec((w.shape[0], W), lambda b, g, c: (0, g))
    sblk = pl.BlockSpec((1, PB, LANES, LANES), lambda b, g, c: (b, g, 0, 0))
    mu2 = row(mu)
    PW = 3 * W + LW
    y, s_out, last = pl.pallas_call(
        functools.partial(_rwkv_kernel, PB=PB, L=L, win_w=win_w, win_a=win_a, win_g=win_g),
        grid=(B, ng, nc),
        in_specs=[seg(0), seg(1), seg(2), pl.BlockSpec((Tb, LW), lambda b, g, c: (r0 + b * nc + c, lb)),
                  seg_prev(0), seg_prev(1), seg_prev(2), pl.BlockSpec((1, 1, LW), lambda b, g, c: (b, 0, lb)),
                  seg_mu(0), seg_mu(1), seg_mu(2), pl.BlockSpec((1, LW), lambda b, g, c: (0, lb)),
                  vec, vec, vec, vec, vec, wblk(wd), wblk(wa), wblk(wg), vec, vec, sblk],
        out_specs=[pl.BlockSpec((Tb, W), lambda b, g, c: (b * nc + c, g)), sblk,
                   pl.BlockSpec((1, 1, 1, PW), lambda b, g, c: (b, g, 0, 0))],
        out_shape=[jax.ShapeDtypeStruct((B * T, RW), BF16),
                   jax.ShapeDtypeStruct((B, npair, LANES, LANES), F32),
                   jax.ShapeDtypeStruct((B, ng, 1, PW), F32)],
        scratch_shapes=[pltpu.VMEM((PB, LANES, LANES), F32), pltpu.VMEM((1, PW), F32)],
        compiler_params=_cp(("parallel", "parallel", "arbitrary")), name="rwkv",
    )(proj, proj, proj, proj, shift_prev, shift_prev, shift_prev, shift_prev, mu2, mu2, mu2, mu2,
      row(w0), row(a0), row(k_k), row(k_a), row(r_k), wd, wa, wg, row(lnx_g), row(lnx_b), s0_bd)
    segs = [last[:, :, 0, s * W:(s + 1) * W].reshape(B, RW) for s in range(3)]
    shift_out = jnp.concatenate(segs + [last[:, 0, 0, 3 * W:]], axis=1)
    return y, s_out, shift_out


def _to_blockdiag(s):
    B, H = s.shape[:2]
    s = s.reshape(B, H // 2, 2, HEAD, HEAD)
    z = jnp.zeros_like(s[:, :, 0])
    top = jnp.concatenate([s[:, :, 0], z], axis=-1)
    bot = jnp.concatenate([z, s[:, :, 1]], axis=-1)
    return jnp.concatenate([top, bot], axis=-2)


def _from_blockdiag(sbd):
    B, P = sbd.shape[:2]
    return jnp.stack([sbd[:, :, :HEAD, :HEAD], sbd[:, :, HEAD:, HEAD:]], axis=2).reshape(B, 2 * P, HEAD, HEAD)


def _qknorm_kernel(q_ref, k_ref, v_ref, c_ref, s1_ref, s2_ref, qg_ref, kg_ref,
                   qb_o, kf_o, kb_o, vf_o, vb_o):
    cos, s1, s2 = c_ref[...], s1_ref[...], s2_ref[...]
    ones2 = _head_ones()

    def norm_rope(x, gain):
        ms = _head_allsum(x * x, ones2) * (1.0 / HEAD)
        y = x * lax.rsqrt(ms + RMS_EPS) * gain
        half = ROT_DIM // 2
        return y * cos + pltpu.roll(y, LANES - half, axis=1) * s1 + pltpu.roll(y, half, axis=1) * s2

    for c in range(0, q_ref.shape[1], LANES):
        cs = slice(c, c + LANES)
        q = norm_rope(q_ref[:, cs], qg_ref[...])
        k = norm_rope(k_ref[:, cs], kg_ref[...])
        qb_o[:, cs] = (q * (HEAD ** -0.5)).astype(qb_o.dtype)
        kf_o[:, cs] = k
        kb_o[:, cs] = k.astype(kb_o.dtype)
    v = v_ref[...]
    vf_o[...] = v
    vb_o[...] = v.astype(vb_o.dtype)


def _rope_tables(pos):
    half = ROT_DIM // 2
    inv_freq = jnp.power(jnp.float32(ROPE_THETA), -jnp.arange(0, ROT_DIM, 2, dtype=jnp.float32) / ROT_DIM)
    ang = pos.astype(jnp.float32)[:, None] * inv_freq[None, :]
    cos, sin = jnp.cos(ang), jnp.sin(ang)
    T = pos.shape[0]
    pad = jnp.zeros((T, HEAD - ROT_DIM), F32)
    z = jnp.zeros((T, half), F32)
    c = jnp.concatenate([cos, cos, pad + 1.0], axis=1)
    s1 = jnp.concatenate([-sin, z, pad], axis=1)
    s2 = jnp.concatenate([z, sin, pad], axis=1)
    tile = lambda t: jnp.concatenate([t, t], axis=1)
    return tile(c), tile(s1), tile(s2)


def _qknorm(proj, row_off, rows, DW, tables, q_norm_g, k_norm_g):
    tr = _pick(rows, 256, 16)
    cw = _pick(DW, 512, LANES)
    assert row_off % tr == 0
    r0 = row_off // tr
    nc = DW // cw
    gain = lambda g: jnp.concatenate([g, g]).reshape(1, LANES)
    seg = lambda s: pl.BlockSpec((tr, cw), lambda i, j: (r0 + i, s * nc + j))
    tab = pl.BlockSpec((tr, LANES), lambda i, j: (i, 0))
    vec = pl.BlockSpec((1, LANES), lambda i, j: (0, 0))
    out = pl.BlockSpec((tr, cw), lambda i, j: (i, j))
    sds = lambda dt: jax.ShapeDtypeStruct((rows, DW), dt)
    return pl.pallas_call(
        _qknorm_kernel, grid=(rows // tr, nc),
        in_specs=[seg(0), seg(1), seg(2), tab, tab, tab, vec, vec],
        out_specs=[out] * 5,
        out_shape=[sds(BF16), sds(F32), sds(BF16), sds(F32), sds(BF16)],
        compiler_params=_cp(("parallel", "parallel")), name="qk_norm_rope",
    )(proj, proj, proj, *tables, gain(q_norm_g), gain(k_norm_g))


def _lambda(lam_ref, lam_init):
    lam = lam_ref[...]
    return (jnp.exp(jnp.sum(lam[0:1] * lam[1:2], axis=1, keepdims=True))
            - jnp.exp(jnp.sum(lam[2:3] * lam[3:4], axis=1, keepdims=True)) + lam_init)


def _stack_components(q):
    lane = lax.broadcasted_iota(jnp.int32, q.shape, 1)
    zero = jnp.zeros_like(q)
    return jnp.concatenate([jnp.where(lane < HEAD, q, zero), jnp.where(lane >= HEAD, q, zero)], axis=0)


def _softmax_step(qq, k, v, m_ref, l_ref, acc_ref, mask):
    s = lax.dot_general(qq, k, (((2,), (2,)), ((0,), (0,))), preferred_element_type=F32)
    if mask is not None:
        s = jnp.where(mask, s, NEG_INF)
    m_old = m_ref[...]
    m_new = jnp.maximum(m_old, jnp.max(s, axis=2, keepdims=True))
    alpha = jnp.exp(m_old - m_new)
    tk = s.shape[2]
    p = jnp.exp(s - jnp.concatenate([m_new] * pl.cdiv(tk, LANES), axis=2)[:, :, :tk])
    v1 = jnp.concatenate([v, jnp.ones(v.shape, v.dtype)], axis=2)
    pv = lax.dot_general(p.astype(BF16), v1, (((2,), (1,)), ((0,), (0,))), preferred_element_type=F32)
    l_ref[...] = alpha * l_ref[...] + pv[:, :, LANES:]
    acc_ref[...] = alpha * acc_ref[...] + pv[:, :, :LANES]
    m_ref[...] = m_new


def _finish_head(t, m_ref, l_ref, acc_ref, lam, g, lam_init):
    o = acc_ref[...] / l_ref[...]
    o = o[:t] - lam * o[t:]
    ms = jnp.mean(o * o, axis=-1, keepdims=True)
    return o * lax.rsqrt(ms + RMS_EPS) * g * (1.0 - lam_init)


def _attn_prompt_kernel(lam_ref, g_ref, q_ref, k_ref, v_ref, o_ref, m_ref, l_ref, acc_ref, *, lam_init):
    qi = pl.program_id(2)
    tq = q_ref.shape[0]
    nh = q_ref.shape[1] // LANES
    cols = [slice(h * LANES, (h + 1) * LANES) for h in range(nh)]
    qq = jnp.stack([_stack_components(q_ref[:, cs]) for cs in cols])
    m_ref[...] = jnp.full(m_ref.shape, NEG_INF, F32)
    l_ref[...] = jnp.zeros(l_ref.shape, F32)
    acc_ref[...] = jnp.zeros(acc_ref.shape, F32)

    def block(j, mask):
        rows = pl.ds(pl.multiple_of(j * tq, tq), tq)
        _softmax_step(qq, jnp.stack([k_ref[rows, cs] for cs in cols]), jnp.stack([v_ref[rows, cs] for cs in cols]),
                      m_ref, l_ref, acc_ref, mask)

    def body(j, carry):
        block(j, None)
        return carry

    lax.fori_loop(0, qi, body, 0)
    r = lax.broadcasted_iota(jnp.int32, (2 * tq, tq), 0)
    cidx = lax.broadcasted_iota(jnp.int32, (2 * tq, tq), 1)
    r = jnp.where(r >= tq, r - tq, r)
    block(qi, (r // CHUNK) >= (cidx // CHUNK))
    lam = _lambda(lam_ref, lam_init)
    for h, cs in enumerate(cols):
        o_ref[:, cs] = _finish_head(tq, m_ref.at[h], l_ref.at[h], acc_ref.at[h], lam, g_ref[...],
                                    lam_init).astype(o_ref.dtype)


def _attn_prompt(qb, kb, vb, lam_vecs, subln_g, B, T, lam_init):
    DW = qb.shape[1]
    nh = DW // LANES
    hg = math.gcd(nh, ATTN_HEADS)
    tq = _pick(T, 256, CHUNK)
    nq = T // tq
    return pl.pallas_call(
        functools.partial(_attn_prompt_kernel, lam_init=lam_init), grid=(B, nh // hg, nq),
        in_specs=[pl.BlockSpec(lam_vecs.shape, lambda b, h, i: (0, 0)),
                  pl.BlockSpec((1, LANES), lambda b, h, i: (0, 0)),
                  pl.BlockSpec((tq, hg * LANES), lambda b, h, i: (b * nq + i, h)),
                  pl.BlockSpec((T, hg * LANES), lambda b, h, i: (b, h)),
                  pl.BlockSpec((T, hg * LANES), lambda b, h, i: (b, h))],
        out_specs=pl.BlockSpec((tq, hg * LANES), lambda b, h, i: (b * nq + i, h)),
        out_shape=jax.ShapeDtypeStruct((B * T, DW), BF16),
        scratch_shapes=[pltpu.VMEM((hg, 2 * tq, LANES), F32)] * 3,
        compiler_params=_cp(("parallel", "parallel", "arbitrary")), name="attn_prompt",
    )(lam_vecs, subln_g.reshape(1, LANES), qb, kb, vb)


def _attn_sample_kernel(lam_ref, g_ref, q_ref, kc_ref, vc_ref, kn_ref, vn_ref, o_ref, m_ref, l_ref, acc_ref,
                        *, lam_init, past):
    j = pl.program_id(1)
    ts = q_ref.shape[0]
    nh = q_ref.shape[1] // LANES

    @pl.when(j == 0)
    def _():
        m_ref[...] = jnp.full(m_ref.shape, NEG_INF, F32)
        l_ref[...] = jnp.zeros(l_ref.shape, F32)
        acc_ref[...] = jnp.zeros(acc_ref.shape, F32)

    cols = [slice(h * LANES, (h + 1) * LANES) for h in range(nh)]
    tk = kc_ref.shape[0] // nh

    def heads(head_k, head_v, mask):
        qq = jnp.stack([_stack_components(q_ref[:, cs]) for cs in cols])
        _softmax_step(qq, jnp.stack([head_k(h).astype(BF16) for h in range(nh)]),
                      jnp.stack([head_v(h).astype(BF16) for h in range(nh)]), m_ref, l_ref, acc_ref, mask)

    heads(lambda h: kc_ref[pl.ds(h, tk, stride=nh), :], lambda h: vc_ref[pl.ds(h, tk, stride=nh), :], None)

    @pl.when(j == pl.num_programs(1) - 1)
    def _():
        r = lax.broadcasted_iota(jnp.int32, (2 * ts, ts), 0)
        cidx = lax.broadcasted_iota(jnp.int32, (2 * ts, ts), 1)
        r = jnp.where(r >= ts, r - ts, r)
        heads(lambda h: kn_ref[:, cols[h]], lambda h: vn_ref[:, cols[h]],
              ((past + r) // CHUNK) >= ((past + cidx) // CHUNK))
        lam = _lambda(lam_ref, lam_init)
        for h in range(nh):
            cs = slice(h * LANES, (h + 1) * LANES)
            o_ref[:, cs] = _finish_head(ts, m_ref.at[h], l_ref.at[h], acc_ref.at[h], lam, g_ref[...],
                                        lam_init).astype(o_ref.dtype)


def _attn_sample(qb, kb, vb, cache_k, cache_v, lam_vecs, subln_g, B, Ts, lam_init):
    DW = qb.shape[1]
    nh = DW // LANES
    past = cache_k.shape[0] // (B * nh)
    tk = _pick(past, 512, 16)
    nkv = past // tk
    new = pl.BlockSpec((Ts, DW), lambda b, j: (b, 0))
    old = pl.BlockSpec((tk * nh, LANES), lambda b, j: (b * nkv + j, 0))
    return pl.pallas_call(
        functools.partial(_attn_sample_kernel, lam_init=lam_init, past=past), grid=(B, nkv),
        in_specs=[pl.BlockSpec(lam_vecs.shape, lambda b, j: (0, 0)),
                  pl.BlockSpec((1, LANES), lambda b, j: (0, 0)),
                  new, old, old, new, new],
        out_specs=new,
        out_shape=jax.ShapeDtypeStruct((B * Ts, DW), BF16),
        scratch_shapes=[pltpu.VMEM((nh, 2 * Ts, LANES), F32)] * 3,
        compiler_params=_cp(("parallel", "arbitrary")), name="attn_sample",
    )(lam_vecs, subln_g.reshape(1, LANES), qb, cache_k, cache_v, kb, vb)


def _layer(i, x_prompt, x_sample, pe_prompt, pe_sample, cache_k, cache_v, state_rwkv, state_shift, w):
    Bp, Tp, D = x_prompt.shape
    Bs, Ts, _ = x_sample.shape
    Mp, Ms = Bp * Tp, Bs * Ts
    past = cache_k.shape[1]
    RW = w["w0"].shape[-1]
    DW = w["w_branch_diff"].shape[0]
    SW = state_shift.shape[-1]
    SWp = _round_up(SW, SEG_ALIGN)
    w_in = w["w_in"]
    lam_init = 0.8 - 0.6 * math.exp(-0.3 * i)

    x_all, h1 = _rmsnorm_cat(x_prompt.reshape(Mp, D), x_sample.reshape(Ms, D), w["norm1_g"])
    w_t = w_in.T
    proj_r = _mm_nt_rows(h1, w_t, 0, SWp, name="proj_rwkv")
    proj_a = _mm_nt_rows(h1, w_t, SW, w_t.shape[0] - SW, name="proj_attn")

    pad_sw = lambda a: jnp.pad(a, ((0, 0), (0, 0), (0, SWp - SW)))
    mu = jnp.pad(w["tm_mu"], (0, SWp - SW))
    lam_vecs = jnp.stack([w["lam_q1"], w["lam_k1"], w["lam_q2"], w["lam_k2"]])
    rwkv_w = (mu, w["w0"], w["a0"], w["k_k"], w["k_a"], w["r_k"].reshape(-1), w["w_decay_up"], w["a_up"], w["g_up"],
              w["lnx_g"], w["lnx_b"])

    s0_p = jnp.zeros((Bp, RW // LANES, LANES, LANES), F32)
    yr_p, s_p, last_p = _rwkv(proj_r, 0, Bp, Tp, SWp, jnp.zeros((Bp, 1, SWp), F32), s0_p, *rwkv_w)
    tab_p = _rope_tables(jnp.arange(Tp, dtype=jnp.int32))
    tab_p = tuple(jnp.tile(t, (Bp, 1)) for t in tab_p)
    qb_p, kf_p, kb_p, vf_p, vb_p = _qknorm(proj_a, 0, Mp, DW, tab_p, w["q_norm_g"], w["k_norm_g"])
    yd_p = _attn_prompt(qb_p, kb_p, vb_p, lam_vecs, w["subln_g"], Bp, Tp, lam_init)

    yr_s, s_s, last_s = _rwkv(proj_r, Mp, Bs, Ts, SWp, pad_sw(state_shift), _to_blockdiag(state_rwkv), *rwkv_w)
    tab_s = _rope_tables(past + jnp.arange(Ts, dtype=jnp.int32))
    tab_s = tuple(jnp.tile(t, (Bs, 1)) for t in tab_s)
    qb_s, kf_s, kb_s, vf_s, vb_s = _qknorm(proj_a, Mp, Ms, DW, tab_s, w["q_norm_g"], w["k_norm_g"])
    yd_s = _attn_sample(qb_s, kb_s, vb_s, cache_k.reshape(-1, LANES), cache_v.reshape(-1, LANES),
                        lam_vecs, w["subln_g"], Bs, Ts, lam_init)

    y_r = jnp.concatenate([yr_p, yr_s], axis=0)
    y_d = jnp.concatenate([yd_p, yd_s], axis=0)
    merged = _merge(y_r, y_d, w["w_branch_rwkv"], w["w_branch_diff"], proj_a, 3 * DW)
    x1 = _mm_fullk(_mm_res_kernel, merged, w["w_out"], F32, res=x_all, tn_target=256, name="out_proj")
    h2 = _rmsnorm(x1, w["norm2_g"])
    u = _mm_fullk(_mm_relu2_kernel, h2, w["w_up"], BF16, name="ffn_up")
    x2 = _mm_res_kloop(u, w["w_down"], x1, name="ffn_down")
    h3 = _rmsnorm(x2, w["ple_norm_g"])
    y_p = _ple(h3, x2, pe_prompt.reshape(Mp, -1), w["w_ple_gate"], w["w_ple_proj"], 0, Mp)
    y_s = _ple(h3, x2, pe_sample.reshape(Ms, -1), w["w_ple_gate"], w["w_ple_proj"], Mp, Ms)

    nh_d = DW // LANES
    shift_p = last_p[:, :SW].reshape(Bp, 1, SW)
    shift_s = last_s[:, :SW].reshape(Bs, 1, SW)
    return (y_p.reshape(Bp, Tp, D), y_s.reshape(Bs, Ts, D),
            kf_p.reshape(Bp, Tp, nh_d, LANES), vf_p.reshape(Bp, Tp, nh_d, LANES), _from_blockdiag(s_p), shift_p,
            kf_s.reshape(Bs, Ts, nh_d, LANES), vf_s.reshape(Bs, Ts, nh_d, LANES), _from_blockdiag(s_s), shift_s)


def kernel(x_prompt, x_sample, p_prompt, p_sample, cache_k, cache_v, state_rwkv, state_shift, norm1_g, w_in, tm_mu, w_decay_up, w0, a_up, a0, g_up, k_k, k_a, r_k, lnx_g, lnx_b, q_norm_g, k_norm_g, lam_q1, lam_k1, lam_q2, lam_k2, subln_g, w_branch_rwkv, w_branch_diff, w_out, norm2_g, w_up, w_down, ple_norm_g, w_ple_gate, w_ple_proj):
    weights = dict(norm1_g=norm1_g, w_in=w_in, tm_mu=tm_mu, w_decay_up=w_decay_up, w0=w0, a_up=a_up, a0=a0,
                   g_up=g_up, k_k=k_k, k_a=k_a, r_k=r_k, lnx_g=lnx_g, lnx_b=lnx_b, q_norm_g=q_norm_g,
                   k_norm_g=k_norm_g, lam_q1=lam_q1, lam_k1=lam_k1, lam_q2=lam_q2, lam_k2=lam_k2,
                   subln_g=subln_g, w_branch_rwkv=w_branch_rwkv, w_branch_diff=w_branch_diff, w_out=w_out,
                   norm2_g=norm2_g, w_up=w_up, w_down=w_down, ple_norm_g=ple_norm_g, w_ple_gate=w_ple_gate,
                   w_ple_proj=w_ple_proj)
    depth = w_in.shape[0]
    hp, hs = x_prompt, x_sample
    outs = [[] for _ in range(8)]
    for i in range(depth):
        res = _layer(i, hp, hs, p_prompt[i], p_sample[i], cache_k[i], cache_v[i], state_rwkv[i], state_shift[i],
                     {name: val[i] for name, val in weights.items()})
        hp, hs = res[0], res[1]
        for acc, val in zip(outs, res[2:]):
            acc.append(val)
    return (hp, hs) + tuple(jnp.stack(acc, 0) for acc in outs)
```
